```python
import math
import jax, jax.numpy as jnp
from jax import lax
import numpy as np

D_MODEL = 1024
BATCH = 8
SEQ = 2048
DEPTH = 2
DEC_BATCH = 32
DEC_SEQ = 1
PAST_LEN = 8192
PAGE_SIZE = 128

N_MIXERS = 2
N_SSM_LAYERS = (DEPTH + 1) // 2
N_ATT_LAYERS = DEPTH // 2
EPS = 1e-6

SSM_EXPAND = 2
SSM_D_INNER = SSM_EXPAND * D_MODEL
SSM_HEAD_DIM = 64
SSM_HEADS = SSM_D_INNER // SSM_HEAD_DIM
SSM_GROUPS = 8
SSM_HPG = SSM_HEADS // SSM_GROUPS
D_STATE = 128
CONV_W = 4
CONV_DIM = SSM_D_INNER + 2 * SSM_GROUPS * D_STATE
SSM_IN_DIM = SSM_D_INNER + CONV_DIM + SSM_HEADS
CHUNK = 128

ATT_WIDTH = 2 * D_MODEL
ATT_HEAD_DIM = 64
ATT_HEADS = ATT_WIDTH // (2 * ATT_HEAD_DIM)
ATT_V_DIM = 2 * ATT_HEAD_DIM
ATT_IN_DIM = 4 * ATT_WIDTH
ATT_SCALE = ATT_HEAD_DIM ** -0.5
Q_BLOCK = 128

F32 = jnp.float32

kernel_name = 'hybrid_ssd_diffattn_step'


def _rms(x, gain):
    xf = x.astype(F32)
    y = xf * lax.rsqrt(jnp.mean(xf * xf, axis=-1, keepdims=True) + EPS)
    return (y * gain.astype(F32)).astype(x.dtype)


def _adaln(c, w, b):
    mod = (jax.nn.silu(c) @ w + b)[:, None, :]
    return jnp.split(mod, 3, axis=-1)


def _causal_conv(xbc, buf, w, b):
    L = xbc.shape[1]
    xp = jnp.concatenate([buf.astype(xbc.dtype), xbc], axis=1)
    y = b
    for k in range(CONV_W):
        y = y + xp[:, k:k + L] * w[k]
    return jax.nn.silu(y), xp[:, L:]


def _ssd(xh, dt, a, bm, cm, h0):
    dtype = xh.dtype
    b, L = xh.shape[:2]
    q = CHUNK if L >= CHUNK else L
    pad = (-L) % q
    if pad:
        pw = lambda t: jnp.pad(t, [(0, 0), (0, pad)] + [(0, 0)] * (t.ndim - 2))
        xh, dt, bm, cm = pw(xh), pw(dt), pw(bm), pw(cm)
    nc = (L + pad) // q
    xc = xh.reshape(b, nc, q, SSM_GROUPS, SSM_HPG, SSM_HEAD_DIM)
    dtc = dt.reshape(b, nc, q, SSM_GROUPS, SSM_HPG)
    bc = bm.reshape(b, nc, q, SSM_GROUPS, D_STATE)
    cc = cm.reshape(b, nc, q, SSM_GROUPS, D_STATE)
    cs = jnp.cumsum(dtc * a, axis=2)
    seg = cs[:, :, :, None] - cs[:, :, None, :]
    causal = jnp.tril(jnp.ones((q, q), bool))[:, :, None, None]
    decay = jnp.exp(jnp.where(causal, seg, -jnp.inf)).astype(dtype)
    xdt = xc * dtc[..., None].astype(dtype)
    cb = jnp.einsum('bctgn,bcsgn->bctsg', cc, bc)
    y_diag = jnp.einsum('bctsgr,bcsgrp->bctgrp', cb[..., None] * decay, xdt)
    to_end = jnp.exp(cs[:, :, -1:] - cs).astype(dtype)
    chunk_states = jnp.einsum('bcsgn,bcsgrp->bcgrpn', bc, xdt * to_end[..., None])
    chunk_decay = jnp.exp(cs[:, :, -1]).astype(dtype)

    def step(h, inp):
        dec, st = inp
        return dec[..., None, None] * h + st, h

    h_last, h_prev = lax.scan(step, h0.astype(dtype),
                              (jnp.moveaxis(chunk_decay, 1, 0), jnp.moveaxis(chunk_states, 1, 0)))
    h_prev = jnp.moveaxis(h_prev, 0, 1)
    y_off = jnp.einsum('bctgn,bcgrpn->bctgrp', cc, h_prev) * jnp.exp(cs)[..., None].astype(dtype)
    y = (y_diag + y_off).reshape(b, nc * q, SSM_GROUPS, SSM_HPG, SSM_HEAD_DIM)[:, :L]
    return y, h_last


def _mamba_mixer(h, conv_buf, ssm_state, w_in, conv_w, conv_b, dt_bias, a_log, d_skip, norm_w, w_out):
    b, L, _ = h.shape
    z, xbc, dt = jnp.split(h @ w_in, [SSM_D_INNER, SSM_D_INNER + CONV_DIM], axis=-1)
    xbc, new_buf = _causal_conv(xbc, conv_buf, conv_w, conv_b)
    xs, bm, cm = jnp.split(xbc, [SSM_D_INNER, SSM_D_INNER + SSM_GROUPS * D_STATE], axis=-1)
    xh = xs.reshape(b, L, SSM_GROUPS, SSM_HPG, SSM_HEAD_DIM)
    bm = bm.reshape(b, L, SSM_GROUPS, D_STATE)
    cm = cm.reshape(b, L, SSM_GROUPS, D_STATE)
    dt = jax.nn.softplus((dt + dt_bias).astype(F32)).reshape(b, L, SSM_GROUPS, SSM_HPG)
    a = -jnp.exp(a_log.astype(F32)).reshape(SSM_GROUPS, SSM_HPG)
    h0 = ssm_state.reshape(b, SSM_GROUPS, SSM_HPG, SSM_HEAD_DIM, D_STATE)
    y, h_last = _ssd(xh, dt, a, bm, cm, h0)
    y = y + xh * d_skip.reshape(SSM_GROUPS, SSM_HPG)[..., None]
    y = y.reshape(b, L, SSM_D_INNER) * jax.nn.silu(z)
    y = _rms(y.reshape(b, L, SSM_GROUPS, SSM_D_INNER // SSM_GROUPS),
             norm_w.reshape(SSM_GROUPS, SSM_D_INNER // SSM_GROUPS)).reshape(b, L, SSM_D_INNER)
    return y @ w_out, new_buf, h_last.reshape(b, SSM_HEADS, SSM_HEAD_DIM, D_STATE)


def _diff_qkv(h, w_in, q_norm, k_norm):
    b, L, _ = h.shape
    q, k, v, g = jnp.split(h @ w_in, 4, axis=-1)
    q = _rms(q.reshape(b, L, ATT_HEADS, 2, ATT_HEAD_DIM), q_norm)
    k = _rms(k.reshape(b, L, ATT_HEADS, 2, ATT_HEAD_DIM), k_norm)
    v = v.reshape(b, L, ATT_HEADS, ATT_V_DIM)
    return q, k, v, g


def _logits(q, k, q_pos, k_pos, slopes):
    s = jnp.einsum('bqhcd,bkhcd->bhcqk', q, k).astype(F32) * ATT_SCALE
    dist = q_pos[:, None] - k_pos[None, :]
    s = s - slopes[:, None, None, None] * dist.astype(F32)
    return jnp.where(dist >= 0, s, -jnp.inf)


def _diff_weights(s, lam):
    p = jax.nn.softmax(s, axis=-1)
    return p[:, :, 0] - lam * p[:, :, 1]


def _prompt_attention(q, k, v, slopes, lam):
    b, L = q.shape[:2]
    nb = L // Q_BLOCK
    qb = jnp.moveaxis(q.reshape(b, nb, Q_BLOCK, ATT_HEADS, 2, ATT_HEAD_DIM), 1, 0)
    k_pos = jnp.arange(L, dtype=jnp.int32)
    q_pos_b = k_pos.reshape(nb, Q_BLOCK)

    def block(args):
        qi, pos = args
        w = _diff_weights(_logits(qi, k, pos, k_pos, slopes), lam)
        return jnp.einsum('bhqk,bkhe->bqhe', w.astype(v.dtype), v)

    o = lax.map(block, (qb, q_pos_b))
    return jnp.moveaxis(o, 0, 1).reshape(b, L, ATT_HEADS, ATT_V_DIM)


def _sample_attention(q, k_new, v_new, k_past, v_past, slopes, lam):
    ds = q.shape[1]
    past = k_past.shape[1]
    q_pos = past + jnp.arange(ds, dtype=jnp.int32)
    s = jnp.concatenate([_logits(q, k_past, q_pos, jnp.arange(past, dtype=jnp.int32), slopes),
                         _logits(q, k_new, q_pos, q_pos, slopes)], axis=-1)
    w = _diff_weights(s, lam).astype(v_new.dtype)
    return (jnp.einsum('bhqk,bkhe->bqhe', w[..., :past], v_past)
            + jnp.einsum('bhqk,bkhe->bqhe', w[..., past:], v_new))


def _diff_out(o, g, subln_w, lam_init, w_out):
    b, L = o.shape[:2]
    o = _rms(o, subln_w) * (1.0 - lam_init)
    o = o.reshape(b, L, ATT_WIDTH) * jax.nn.silu(g)
    return o @ w_out


def setup_inputs(seed: int = 0) -> dict:
    key = jax.random.key(seed)
    ks = jax.random.split(key, 32)
    n_pages = PAST_LEN // PAGE_SIZE
    n_used = DEC_BATCH * n_pages
    n_phys = n_used + n_used // 4

    def nrm(k, shape, s=1.0):
        return s * jax.random.normal(k, shape, F32)

    page_table = jax.random.permutation(ks[6], n_phys)[:n_used].reshape(DEC_BATCH, n_pages).astype(jnp.int32)
    dt0 = jnp.exp(jax.random.uniform(ks[15], (N_SSM_LAYERS, SSM_HEADS), F32, math.log(1e-3), math.log(1e-1)))
    return {
        'x_prompt': nrm(ks[0], (BATCH, SEQ, D_MODEL)),
        'x_sample': nrm(ks[1], (DEC_BATCH, DEC_SEQ, D_MODEL)),
        'state_conv': nrm(ks[2], (N_SSM_LAYERS, DEC_BATCH, CONV_W - 1, CONV_DIM)),
        'state_ssm': nrm(ks[3], (N_SSM_LAYERS, DEC_BATCH, SSM_HEADS, SSM_HEAD_DIM, D_STATE), 0.1),
        'cache_k': nrm(ks[4], (N_ATT_LAYERS, n_phys, PAGE_SIZE, ATT_HEADS, 2, ATT_HEAD_DIM)),
        'cache_v': nrm(ks[5], (N_ATT_LAYERS, n_phys, PAGE_SIZE, ATT_HEADS, ATT_V_DIM)),
        'page_table': page_table,
        'c_prompt': nrm(ks[7], (BATCH, D_MODEL)),
        'c_sample': nrm(ks[8], (DEC_BATCH, D_MODEL)),
        'ada_w': nrm(ks[9], (DEPTH, D_MODEL, 3 * D_MODEL), 0.5 * D_MODEL ** -0.5),
        'ada_b': nrm(ks[10], (DEPTH, 3 * D_MODEL), 0.01),
        'ln_w': 1.0 + nrm(ks[11], (DEPTH, D_MODEL), 0.01),
        'ssm_w_in': nrm(ks[12], (N_SSM_LAYERS, D_MODEL, SSM_IN_DIM), D_MODEL ** -0.5),
        'ssm_conv_w': nrm(ks[13], (N_SSM_LAYERS, CONV_W, CONV_DIM), CONV_W ** -0.5),
        'ssm_conv_b': nrm(ks[14], (N_SSM_LAYERS, CONV_DIM), 0.01),
        'ssm_dt_bias': dt0 + jnp.log(-jnp.expm1(-dt0)),
        'ssm_a_log': jnp.log(jax.random.uniform(ks[16], (N_SSM_LAYERS, SSM_HEADS), F32, 1.0, 16.0)),
        'ssm_d': 1.0 + nrm(ks[17], (N_SSM_LAYERS, SSM_HEADS), 0.01),
        'ssm_norm_w': 1.0 + nrm(ks[18], (N_SSM_LAYERS, SSM_D_INNER), 0.01),
        'ssm_w_out': nrm(ks[19], (N_SSM_LAYERS, SSM_D_INNER, D_MODEL), SSM_D_INNER ** -0.5),
        'att_w_in': nrm(ks[20], (N_ATT_LAYERS, D_MODEL, ATT_IN_DIM), D_MODEL ** -0.5),
        'att_q_norm': 1.0 + nrm(ks[21], (N_ATT_LAYERS, ATT_HEAD_DIM), 0.01),
        'att_k_norm': 1.0 + nrm(ks[22], (N_ATT_LAYERS, ATT_HEAD_DIM), 0.01),
        'att_lambda': nrm(ks[23], (N_ATT_LAYERS, 4, ATT_HEAD_DIM), 0.1),
        'att_subln_w': 1.0 + nrm(ks[24], (N_ATT_LAYERS, ATT_V_DIM), 0.01),
        'att_w_out': nrm(ks[25], (N_ATT_LAYERS, ATT_WIDTH, D_MODEL), ATT_WIDTH ** -0.5),
    }


def reference(x_prompt, x_sample, state_conv, state_ssm, cache_k, cache_v, page_table, c_prompt, c_sample,
              ada_w, ada_b, ln_w, ssm_w_in, ssm_conv_w, ssm_conv_b, ssm_dt_bias, ssm_a_log, ssm_d,
              ssm_norm_w, ssm_w_out, att_w_in, att_q_norm, att_k_norm, att_lambda, att_subln_w, att_w_out):
    slopes = jnp.exp2(-8.0 * jnp.arange(1, ATT_HEADS + 1, dtype=F32) / ATT_HEADS)
    dec_batch = x_sample.shape[0]
    yp, ys = x_prompt, x_sample
    conv_p, ssm_p, k_p, v_p = [], [], [], []
    conv_s, ssm_s, k_s, v_s = [], [], [], []
    for i in range(DEPTH):
        kind, j = i % N_MIXERS, i // N_MIXERS
        shift_p, scale_p, gate_p = _adaln(c_prompt, ada_w[i], ada_b[i])
        shift_s, scale_s, gate_s = _adaln(c_sample, ada_w[i], ada_b[i])
        hp = _rms(yp, ln_w[i]) * (1.0 + scale_p) + shift_p
        hs = _rms(ys, ln_w[i]) * (1.0 + scale_s) + shift_s
        if kind == 0:
            prm = (ssm_w_in[j], ssm_conv_w[j], ssm_conv_b[j], ssm_dt_bias[j], ssm_a_log[j], ssm_d[j],
                   ssm_norm_w[j], ssm_w_out[j])
            zero_buf = jnp.zeros((hp.shape[0], CONV_W - 1, CONV_DIM), hp.dtype)
            zero_h = jnp.zeros((hp.shape[0], SSM_HEADS, SSM_HEAD_DIM, D_STATE), hp.dtype)
            out_p, buf_p, h_p = _mamba_mixer(hp, zero_buf, zero_h, *prm)
            out_s, buf_s, h_s = _mamba_mixer(hs, state_conv[j], state_ssm[j], *prm)
            conv_p.append(buf_p); ssm_p.append(h_p)
            conv_s.append(buf_s); ssm_s.append(h_s)
        else:
            lam_init = 0.8 - 0.6 * math.exp(-0.3 * i)
            lmb = att_lambda[j].astype(F32)
            lam = jnp.exp(jnp.sum(lmb[0] * lmb[1])) - jnp.exp(jnp.sum(lmb[2] * lmb[3])) + lam_init
            qp, kp, vp, zp = _diff_qkv(hp, att_w_in[j], att_q_norm[j], att_k_norm[j])
            out_p = _diff_out(_prompt_attention(qp, kp, vp, slopes, lam), zp, att_subln_w[j], lam_init, att_w_out[j])
            qs, kn, vn, zs = _diff_qkv(hs, att_w_in[j], att_q_norm[j], att_k_norm[j])
            k_past = cache_k[j, page_table].reshape(dec_batch, -1, ATT_HEADS, 2, ATT_HEAD_DIM)
            v_past = cache_v[j, page_table].reshape(dec_batch, -1, ATT_HEADS, ATT_V_DIM)
            out_s = _diff_out(_sample_attention(qs, kn, vn, k_past, v_past, slopes, lam), zs, att_subln_w[j],
                              lam_init, att_w_out[j])
            k_p.append(kp); v_p.append(vp)
            k_s.append(kn); v_s.append(vn)
        yp = yp + gate_p * out_p
        ys = ys + gate_s * out_s
    return (yp, ys, jnp.stack(conv_p), jnp.stack(ssm_p), jnp.stack(k_p), jnp.stack(v_p),
            jnp.stack(conv_s), jnp.stack(ssm_s), jnp.stack(k_s), jnp.stack(v_s))
```

```python
import functools
import math

import jax
import jax.numpy as jnp
from jax import lax
from jax.experimental import pallas as pl
from jax.experimental.pallas import tpu as pltpu

F32 = jnp.float32
BF16 = jnp.bfloat16
EPS = 1e-6
NEG = -1e30

LANES = 128
SSD_CHUNK = 128
SSM_HEAD_DIM = 64
SSM_GROUPS = 8
D_STATE = 128
CONV_W = 4
ATT_HEAD_DIM = 64
ATT_V_DIM = 128
ATT_BLOCK = 256
PAGES_PER_STEP = 4
VMEM_LIMIT = 56 * 1024 * 1024


def _cparams(sem):
    return pltpu.CompilerParams(dimension_semantics=sem, vmem_limit_bytes=VMEM_LIMIT)


def _silu(x):
    return x / (1.0 + jnp.exp(-x))


def _softplus(x):
    return jnp.maximum(x, 0.0) + jnp.log(1.0 + jnp.exp(-jnp.abs(x)))


def _split_dot(v, e):
    hi = v.astype(BF16)
    lo = (v - hi.astype(F32)).astype(BF16)
    return (jnp.dot(hi, e, preferred_element_type=F32)
            + jnp.dot(lo, e, preferred_element_type=F32))


def _norm_mod(x, lnw, scale, shift):
    ms = jnp.mean(x * x, axis=-1, keepdims=True)
    y = x * lax.rsqrt(ms + EPS) * lnw
    return y * (1.0 + scale) + shift


def _adaln_kernel(c_ref, w_ref, b_ref, o_ref):
    s = _silu(c_ref[...])
    o_ref[0] = jnp.dot(s.astype(BF16), w_ref[0].astype(BF16), preferred_element_type=F32) + b_ref[0]


def _adaln(c_all, ada_w, ada_b):
    depth, d, d3 = ada_w.shape
    r = c_all.shape[0]
    return pl.pallas_call(
        _adaln_kernel,
        grid=(depth, d3 // d),
        in_specs=[pl.BlockSpec((r, d), lambda i, j: (0, 0)),
                  pl.BlockSpec((1, d, d), lambda i, j: (i, 0, j)),
                  pl.BlockSpec((1, 1, d), lambda i, j: (i, 0, j))],
        out_specs=pl.BlockSpec((1, r, d), lambda i, j: (i, 0, j)),
        out_shape=jax.ShapeDtypeStruct((depth, r, d3), F32),
        compiler_params=_cparams(("arbitrary", "arbitrary")),
        name="adaln",
    )(c_all, ada_w, ada_b.reshape(depth, 1, d3))


N_CHUNK = 512


def _inproj_ssm_kernel(x_ref, lnw_ref, sc_ref, sh_ref, wz_ref, wx_ref, wd_ref, z_ref, xbc_ref, dt_ref):
    h = _norm_mod(x_ref[...], lnw_ref[...], sc_ref[0], sh_ref[0]).astype(BF16)
    for w_ref, o_ref in ((wz_ref, z_ref), (wx_ref, xbc_ref)):
        for j in range(0, w_ref.shape[1], N_CHUNK):
            o_ref[:, j:j + N_CHUNK] = jnp.dot(h, w_ref[:, j:j + N_CHUNK], preferred_element_type=F32)
    dt_ref[...] = jnp.dot(h, wd_ref[...], preferred_element_type=F32)


def _mod_specs(mod_rows, d, tm, tiles_per_seq):
    if mod_rows == 1:
        return pl.BlockSpec((1, 1, d), lambda i: (i // tiles_per_seq, 0, 0))
    return pl.BlockSpec((1, tm, d), lambda i: (0, i, 0))


def _inproj_ssm(x, lnw, scale, shift, wz, wx, wd, tm, tiles_per_seq):
    m, d = x.shape
    full = lambda a: pl.BlockSpec(a.shape, lambda i: (0,) * a.ndim)
    row = lambda n: pl.BlockSpec((tm, n), lambda i: (i, 0))
    mod = _mod_specs(scale.shape[1], d, tm, tiles_per_seq)
    return pl.pallas_call(
        _inproj_ssm_kernel,
        grid=(m // tm,),
        in_specs=[row(d), full(lnw), mod, mod, full(wz), full(wx), full(wd)],
        out_specs=[row(wz.shape[1]), row(wx.shape[1]), row(wd.shape[1])],
        out_shape=[jax.ShapeDtypeStruct((m, wz.shape[1]), F32),
                   jax.ShapeDtypeStruct((m, wx.shape[1]), F32),
                   jax.ShapeDtypeStruct((m, wd.shape[1]), F32)],
        compiler_params=_cparams(("arbitrary",)),
        name="inproj_ssm",
    )(x, lnw, scale, shift, wz, wx, wd)


def _inproj_att_kernel(x_ref, lnw_ref, sc_ref, sh_ref, wq_ref, wk_ref, wv_ref, wg_ref, qn_ref, kn_ref,
                       gsum_ref, q_ref, k_ref, v_ref, g_ref, *, q_scale):
    h = _norm_mod(x_ref[...], lnw_ref[...], sc_ref[0], sh_ref[0]).astype(BF16)
    gsum = gsum_ref[...]
    width = gsum.shape[0]
    inv_hd = 1.0 / ATT_HEAD_DIM

    def head_rms(t, gain):
        ms = _split_dot(t * t, gsum) * inv_hd
        return t * lax.rsqrt(ms + EPS) * gain

    for j in range(0, wq_ref.shape[1], N_CHUNK):
        qc = jnp.dot(h, wq_ref[:, j:j + N_CHUNK], preferred_element_type=F32)
        kc = jnp.dot(h, wk_ref[:, j:j + N_CHUNK], preferred_element_type=F32)
        for i in range(0, N_CHUNK, width):
            qn = head_rms(qc[:, i:i + width], qn_ref[:, j + i:j + i + width])
            q_ref[:, j + i:j + i + width] = (qn * q_scale).astype(q_ref.dtype)
            k_ref[:, j + i:j + i + width] = head_rms(kc[:, i:i + width], kn_ref[:, j + i:j + i + width])
        v_ref[:, j:j + N_CHUNK] = jnp.dot(h, wv_ref[:, j:j + N_CHUNK], preferred_element_type=F32)
        g_ref[:, j:j + N_CHUNK] = jnp.dot(h, wg_ref[:, j:j + N_CHUNK], preferred_element_type=F32)


def _inproj_att(x, lnw, scale, shift, wq, wk, wv, wg, qn, kn, gsum, tm, tiles_per_seq):
    m, d = x.shape
    n = wq.shape[1]
    full = lambda a: pl.BlockSpec(a.shape, lambda i: (0,) * a.ndim)
    row = lambda c: pl.BlockSpec((tm, c), lambda i: (i, 0))
    mod = _mod_specs(scale.shape[1], d, tm, tiles_per_seq)
    return pl.pallas_call(
        functools.partial(_inproj_att_kernel, q_scale=ATT_HEAD_DIM ** -0.5),
        grid=(m // tm,),
        in_specs=[row(d), full(lnw), mod, mod, full(wq), full(wk), full(wv), full(wg), full(qn), full(kn),
                  full(gsum)],
        out_specs=[row(n), row(n), row(n), row(n)],
        out_shape=[jax.ShapeDtypeStruct((m, n), BF16), jax.ShapeDtypeStruct((m, n), F32),
                   jax.ShapeDtypeStruct((m, n), F32), jax.ShapeDtypeStruct((m, n), F32)],
        compiler_params=_cparams(("arbitrary",)),
        name="inproj_att",
    )(x, lnw, scale, shift, wq, wk, wv, wg, qn, kn, gsum)


def _outproj_kernel(a_ref, w_ref, x_ref, gate_ref, o_ref):
    y = jnp.dot(a_ref[...], w_ref[...], preferred_element_type=F32)
    o_ref[...] = x_ref[...] + gate_ref[0] * y


def _outproj(a, w, xres, gate, tm, tiles_per_seq):
    m, k = a.shape
    d = w.shape[1]
    return pl.pallas_call(
        _outproj_kernel,
        grid=(m // tm,),
        in_specs=[pl.BlockSpec((tm, k), lambda i: (i, 0)),
                  pl.BlockSpec((k, d), lambda i: (0, 0)),
                  pl.BlockSpec((tm, d), lambda i: (i, 0)),
                  _mod_specs(gate.shape[1], d, tm, tiles_per_seq)],
        out_specs=pl.BlockSpec((tm, d), lambda i: (i, 0)),
        out_shape=jax.ShapeDtypeStruct((m, d), F32),
        compiler_params=_cparams(("arbitrary",)),
        name="outproj",
    )(a, w, xres, gate)


def _ssd_prompt_kernel(xbc_ref, z_ref, dt_ref, cw_ref, cb_ref, dtb_ref, alog_ref, dsk_ref, nw_ref,
                       e64_ref, e128_ref, y_ref, st_ref, xp_ref, xc_ref, *, d_inner):
    q = SSD_CHUNK
    conv_dim = xbc_ref.shape[1]
    n_heads = dt_ref.shape[1]
    hpg = n_heads // SSM_GROUPS
    gw = hpg * SSM_HEAD_DIM
    c_idx = pl.program_id(1)

    @pl.when(c_idx == 0)
    def _():
        xp_ref[0:8, :] = jnp.zeros((8, conv_dim), F32)
        st_ref[...] = jnp.zeros(st_ref.shape, F32)

    xp_ref[8:8 + q, :] = xbc_ref[...]
    for j in range(0, conv_dim, N_CHUNK):
        cs_ = slice(j, j + N_CHUNK)
        acc = cb_ref[:, cs_] + cw_ref[3:4, cs_] * xp_ref[8:8 + q, cs_]
        for k in range(CONV_W - 1):
            acc = acc + cw_ref[k:k + 1, cs_] * xp_ref[5 + k:5 + k + q, cs_]
        xc_ref[:, cs_] = _silu(acc)
    xp_ref[0:8, :] = xp_ref[q:q + 8, :]

    dt = _softplus(dt_ref[...] + dtb_ref[...])
    a = -jnp.exp(alog_ref[...])
    dta = dt * a
    ti = lax.broadcasted_iota(jnp.int32, (q, q), 0)
    si = lax.broadcasted_iota(jnp.int32, (q, q), 1)
    causal = si <= ti
    tril = jnp.where(causal, 1.0, 0.0).astype(F32)
    cs = jnp.dot(tril, dta, precision=lax.Precision.HIGHEST, preferred_element_type=F32)
    cs_t = lax.dot_general(dta, tril, (((0,), (1,)), ((), ())), precision=lax.Precision.HIGHEST,
                           preferred_element_type=F32)
    cs_last = cs[q - 1:q, :]
    e64 = e64_ref[...]
    dt_e = _split_dot(dt, e64)
    ecs_e = _split_dot(jnp.exp(cs), e64)
    toend_e = _split_dot(jnp.exp(cs_last - cs), e64)
    dec_rows = _split_dot(jnp.broadcast_to(jnp.exp(cs_last), (8, n_heads)), e128_ref[...])

    lane = lax.broadcasted_iota(jnp.int32, (q, LANES), 1)
    left = lane < SSM_HEAD_DIM
    b_off = d_inner
    c_off = d_inner + SSM_GROUPS * D_STATE
    for g in range(SSM_GROUPS):
        gs = slice(g * gw, (g + 1) * gw)
        xs = xc_ref[:, gs]
        bg = xc_ref[:, b_off + g * D_STATE:b_off + (g + 1) * D_STATE].astype(BF16)
        cg = xc_ref[:, c_off + g * D_STATE:c_off + (g + 1) * D_STATE].astype(BF16)
        xdt = xs * dt_e[:, gs]
        cb = lax.dot_general(cg, bg, (((1,), (1,)), ((), ())), preferred_element_type=F32)
        y_parts = []
        for pr in range(hpg // 2):
            ms = []
            for hh in range(2):
                h = g * hpg + pr * 2 + hh
                seg = cs[:, h:h + 1] - cs_t[h:h + 1, :]
                dec = jnp.exp(jnp.where(causal, seg, NEG))
                ms.append((cb * dec).astype(BF16))
            xpair = xdt[:, pr * LANES:(pr + 1) * LANES]
            rhs = jnp.concatenate([jnp.where(left, xpair, 0.0), jnp.where(left, 0.0, xpair)], axis=0)
            y_parts.append(jnp.dot(jnp.concatenate(ms, axis=1), rhs.astype(BF16), preferred_element_type=F32))
        y = jnp.concatenate(y_parts, axis=1)
        st = st_ref[0, g * gw:(g + 1) * gw, :]
        y_off = lax.dot_general(cg, st.astype(BF16), (((1,), (1,)), ((), ())), preferred_element_type=F32)
        y = y + y_off * ecs_e[:, gs] + xs * dsk_ref[:, gs]
        upd = lax.dot_general((xdt * toend_e[:, gs]).astype(BF16), bg, (((0,), (0,)), ((), ())),
                              preferred_element_type=F32)
        for hh in range(hpg):
            h = g * hpg + hh
            rs = slice(hh * SSM_HEAD_DIM, (hh + 1) * SSM_HEAD_DIM)
            st_ref[0, g * gw + hh * SSM_HEAD_DIM:g * gw + (hh + 1) * SSM_HEAD_DIM, :] = (
                st[rs, :] * dec_rows[0:1, h * D_STATE:(h + 1) * D_STATE] + upd[rs, :])
        y = y * _silu(z_ref[:, gs])
        ms_ = jnp.mean(y * y, axis=-1, keepdims=True)
        y_ref[:, gs] = (y * lax.rsqrt(ms_ + EPS) * nw_ref[:, gs]).astype(y_ref.dtype)


def _ssd_prompt(xbc, z, dt, conv_w, conv_b, dt_bias, a_log, d_skip_e, norm_w, e64, e128, batch, seq):
    m, conv_dim = xbc.shape
    d_inner = z.shape[1]
    n_heads = dt.shape[1]
    nc = seq // SSD_CHUNK
    full = lambda a: pl.BlockSpec(a.shape, lambda b, c: (0,) * a.ndim)
    row = lambda n: pl.BlockSpec((SSD_CHUNK, n), lambda b, c: (b * nc + c, 0))
    return pl.pallas_call(
        functools.partial(_ssd_prompt_kernel, d_inner=d_inner),
        grid=(batch, nc),
        in_specs=[row(conv_dim), row(d_inner), row(n_heads), full(conv_w), full(conv_b), full(dt_bias),
                  full(a_log), full(d_skip_e), full(norm_w), full(e64), full(e128)],
        out_specs=[row(d_inner),
                   pl.BlockSpec((1, n_heads * SSM_HEAD_DIM, D_STATE), lambda b, c: (b, 0, 0))],
        out_shape=[jax.ShapeDtypeStruct((m, d_inner), BF16),
                   jax.ShapeDtypeStruct((batch, n_heads * SSM_HEAD_DIM, D_STATE), F32)],
        scratch_shapes=[pltpu.VMEM((SSD_CHUNK + 8, conv_dim), F32), pltpu.VMEM((SSD_CHUNK, conv_dim), F32)],
        compiler_params=_cparams(("arbitrary", "arbitrary")),
        name="ssd_prompt",
    )(xbc, z, dt, conv_w, conv_b, dt_bias, a_log, d_skip_e, norm_w, e64, e128)


def _ssd_step_kernel(xbc_ref, z_ref, dt_ref, cst_ref, st_ref, cw_ref, cb_ref, dtb_ref, alog_ref, dsk_ref,
                     nw_ref, e64_ref, e128_ref, y_ref, cst_out_ref, st_out_ref, *, d_inner):
    n_heads = dt_ref.shape[2]
    hpg = n_heads // SSM_GROUPS
    gw = hpg * SSM_HEAD_DIM
    x_new = xbc_ref[0]
    prev = cst_ref[0]
    acc = cb_ref[...] + cw_ref[CONV_W - 1:CONV_W, :] * x_new
    for k in range(CONV_W - 1):
        acc = acc + cw_ref[k:k + 1, :] * prev[k:k + 1, :]
    xc = _silu(acc)
    cst_out_ref[0, 0:CONV_W - 2, :] = prev[1:CONV_W - 1, :]
    cst_out_ref[0, CONV_W - 2:CONV_W - 1, :] = x_new

    dt = _softplus(dt_ref[0] + dtb_ref[...])
    a = -jnp.exp(alog_ref[...])
    dec = jnp.exp(dt * a)
    dt8 = jnp.broadcast_to(dt, (8, n_heads))
    dt_e = _split_dot(dt8, e64_ref[...])[0:1, :]
    dec_rows = _split_dot(jnp.broadcast_to(dec, (8, n_heads)), e128_ref[...])
    row0 = lax.broadcasted_iota(jnp.int32, (8, D_STATE), 0) == 0
    b_off = d_inner
    c_off = d_inner + SSM_GROUPS * D_STATE
    for g in range(SSM_GROUPS):
        gs = slice(g * gw, (g + 1) * gw)
        xs = xc[:, gs]
        xdt8 = jnp.broadcast_to(xs * dt_e[:, gs], (8, gw)).astype(BF16)
        bg = xc[:, b_off + g * D_STATE:b_off + (g + 1) * D_STATE]
        cg = xc[:, c_off + g * D_STATE:c_off + (g + 1) * D_STATE]
        b8 = jnp.where(row0, jnp.broadcast_to(bg, (8, D_STATE)), 0.0).astype(BF16)
        c8 = jnp.broadcast_to(cg, (8, D_STATE)).astype(BF16)
        upd = lax.dot_general(xdt8, b8, (((0,), (0,)), ((), ())), preferred_element_type=F32)
        news = []
        for hh in range(hpg):
            h = g * hpg + hh
            rs = slice(g * gw + hh * SSM_HEAD_DIM, g * gw + (hh + 1) * SSM_HEAD_DIM)
            new = (st_ref[0, rs, :] * dec_rows[0:1, h * D_STATE:(h + 1) * D_STATE]
                   + upd[hh * SSM_HEAD_DIM:(hh + 1) * SSM_HEAD_DIM, :])
            st_out_ref[0, rs, :] = new
            news.append(new)
        st_new = jnp.concatenate(news, axis=0).astype(BF16)
        y = lax.dot_general(c8, st_new, (((1,), (1,)), ((), ())), preferred_element_type=F32)[0:1, :]
        y = y + xs * dsk_ref[:, gs]
        y = y * _silu(z_ref[0, :, gs])
        ms_ = jnp.mean(y * y, axis=-1, keepdims=True)
        y_ref[0, :, gs] = (y * lax.rsqrt(ms_ + EPS) * nw_ref[:, gs]).astype(y_ref.dtype)


def _ssd_step(xbc, z, dt, conv_state, ssm_state, conv_w, conv_b, dt_bias, a_log, d_skip_e, norm_w, e64, e128):
    bsz, conv_dim = xbc.shape
    d_inner = z.shape[1]
    n_heads = dt.shape[1]
    rows = n_heads * SSM_HEAD_DIM
    full = lambda a: pl.BlockSpec(a.shape, lambda b: (0,) * a.ndim)
    per_b = lambda *s: pl.BlockSpec((1,) + s, lambda b: (b,) + (0,) * len(s))
    return pl.pallas_call(
        functools.partial(_ssd_step_kernel, d_inner=d_inner),
        grid=(bsz,),
        in_specs=[per_b(1, conv_dim), per_b(1, d_inner), per_b(1, n_heads), per_b(CONV_W - 1, conv_dim),
                  per_b(rows, D_STATE), full(conv_w), full(conv_b), full(dt_bias), full(a_log),
                  full(d_skip_e), full(norm_w), full(e64), full(e128)],
        out_specs=[per_b(1, d_inner), per_b(CONV_W - 1, conv_dim), per_b(rows, D_STATE)],
        out_shape=[jax.ShapeDtypeStruct((bsz, 1, d_inner), BF16),
                   jax.ShapeDtypeStruct((bsz, CONV_W - 1, conv_dim), F32),
                   jax.ShapeDtypeStruct((bsz, rows, D_STATE), F32)],
        compiler_params=_cparams(("arbitrary",)),
        name="ssd_step",
    )(xbc.reshape(bsz, 1, conv_dim), z.reshape(bsz, 1, d_inner), dt.reshape(bsz, 1, n_heads), conv_state,
      ssm_state.reshape(bsz, rows, D_STATE), conv_w, conv_b, dt_bias, a_log, d_skip_e, norm_w, e64, e128)


def _attn_prompt_kernel(slopes_ref, lam_ref, q_ref, k_ref, v_ref, g_ref, sw_ref, o_ref,
                        kb_ref, vb_ref, m1_ref, l1_ref, a1_ref, m2_ref, l2_ref, a2_ref, *, out_scale):
    t = ATT_BLOCK
    seq = q_ref.shape[0]
    slope = slopes_ref[pl.program_id(1)]
    lam = lam_ref[0]
    kb_ref[...] = k_ref[...].astype(BF16)
    vb_ref[...] = v_ref[...].astype(BF16)
    lane = lax.broadcasted_iota(jnp.int32, (t, LANES), 1)
    first = lane < ATT_HEAD_DIM
    causal = lax.broadcasted_iota(jnp.int32, (t, t), 1) <= lax.broadcasted_iota(jnp.int32, (t, t), 0)
    kidx = lax.broadcasted_iota(jnp.int32, (1, t), 1).astype(F32)
    stats = ((m1_ref, l1_ref, a1_ref), (m2_ref, l2_ref, a2_ref))
    nt = (((1,), (1,)), ((), ()))

    def q_block(qi, carry):
        q0 = pl.multiple_of(qi * t, t)
        qp = q_ref[pl.ds(q0, t), :].astype(F32)
        qs = (jnp.where(first, qp, 0.0).astype(BF16), jnp.where(first, 0.0, qp).astype(BF16))

        kd = kb_ref[pl.ds(q0, t), :]
        vd = vb_ref[pl.ds(q0, t), :]
        bias_d = slope * kidx
        for c, (m_ref, l_ref, a_ref) in enumerate(stats):
            s = lax.dot_general(qs[c], kd, nt, preferred_element_type=F32) + bias_d
            s = jnp.where(causal, s, NEG)
            m = jnp.max(s, axis=1, keepdims=True)
            p = jnp.exp(s - m)
            m_ref[...] = jnp.broadcast_to(m, (t, LANES))
            l_ref[...] = jnp.broadcast_to(jnp.sum(p, axis=1, keepdims=True), (t, LANES))
            a_ref[...] = jnp.dot(p.astype(BF16), vd, preferred_element_type=F32)

        def kv_block(j, carry2):
            k0 = pl.multiple_of(j * t, t)
            kj = kb_ref[pl.ds(k0, t), :]
            vj = vb_ref[pl.ds(k0, t), :]
            bias = slope * (kidx + (k0 - q0).astype(F32))
            for c, (m_ref, l_ref, a_ref) in enumerate(stats):
                s = lax.dot_general(qs[c], kj, nt, preferred_element_type=F32) + bias
                m_prev = m_ref[...]
                m_new = jnp.maximum(m_prev, jnp.max(s, axis=1, keepdims=True))
                alpha = jnp.exp(m_prev - m_new)
                p = jnp.exp(s - jnp.concatenate([m_new] * (t // LANES), axis=1))
                l_ref[...] = alpha * l_ref[...] + jnp.sum(p, axis=1, keepdims=True)
                a_ref[...] = alpha * a_ref[...] + jnp.dot(p.astype(BF16), vj, preferred_element_type=F32)
                m_ref[...] = m_new
            return carry2

        lax.fori_loop(0, qi, kv_block, 0)

        o = a1_ref[...] / l1_ref[...] - lam * (a2_ref[...] / l2_ref[...])
        ms = jnp.mean(o * o, axis=-1, keepdims=True)
        o = o * lax.rsqrt(ms + EPS) * sw_ref[...] * out_scale
        o_ref[pl.ds(q0, t), :] = (o * _silu(g_ref[pl.ds(q0, t), :])).astype(o_ref.dtype)
        return carry

    lax.fori_loop(0, seq // t, q_block, 0)


def _attn_prompt(slopes, lam, q, k, v, g, subln_w, batch, seq, out_scale):
    m, width = q.shape
    n_heads = width // ATT_V_DIM
    blk = pl.BlockSpec((seq, ATT_V_DIM), lambda b, h, *_: (b, h))
    t = ATT_BLOCK
    stat = pltpu.VMEM((t, LANES), F32)
    grid_spec = pltpu.PrefetchScalarGridSpec(
        num_scalar_prefetch=2,
        grid=(batch, n_heads),
        in_specs=[blk, blk, blk, blk, pl.BlockSpec((1, ATT_V_DIM), lambda b, h, *_: (0, 0))],
        out_specs=blk,
        scratch_shapes=[pltpu.VMEM((seq, ATT_V_DIM), BF16), pltpu.VMEM((seq, ATT_V_DIM), BF16),
                        stat, stat, stat, stat, stat, stat],
    )
    return pl.pallas_call(
        functools.partial(_attn_prompt_kernel, out_scale=out_scale),
        grid_spec=grid_spec,
        out_shape=jax.ShapeDtypeStruct((m, width), BF16),
        compiler_params=_cparams(("arbitrary", "arbitrary")),
        name="attn_prompt",
    )(slopes, lam, q, k, v, g, subln_w)


def _attn_decode_kernel(pt_ref, lam_ref, *refs, n_pages, past_len, page_size, out_scale):
    del pt_ref
    npp = PAGES_PER_STEP
    k_refs = refs[:npp]
    v_refs = refs[npp:2 * npp]
    (q_ref, kn_ref, vn_ref, g_ref, sw_ref, slope_ref, o_ref, qb_ref, m_ref, l_ref, acc_ref) = refs[2 * npp:]
    s_idx = pl.program_id(1)
    n_hc, width = qb_ref.shape
    n_heads = n_hc // 2
    nt = (((1,), (1,)), ((), ()))

    @pl.when(s_idx == 0)
    def _():
        r = lax.broadcasted_iota(jnp.int32, (n_hc, width), 0)
        ln = lax.broadcasted_iota(jnp.int32, (n_hc, width), 1)
        sel = lax.shift_right_logical(ln, 6) == r
        qrow = jnp.broadcast_to(q_ref[0].astype(F32), (n_hc, width))
        qblk = jnp.where(sel, qrow, 0.0)
        qb_ref[...] = qblk.astype(BF16)
        s_new = jnp.sum(qblk * kn_ref[0], axis=1, keepdims=True)
        m_ref[...] = jnp.broadcast_to(s_new, (n_hc, LANES))
        l_ref[...] = jnp.ones((n_hc, LANES), F32)
        acc_ref[...] = jnp.broadcast_to(vn_ref[0], (n_hc, width))

    qblk = qb_ref[...]
    slope = slope_ref[...]
    kofs = lax.broadcasted_iota(jnp.int32, (1, page_size), 1)
    s_parts = []
    for i in range(npp):
        kp = k_refs[i][0].astype(BF16)
        s = lax.dot_general(qblk, kp, nt, preferred_element_type=F32)
        kpos = (s_idx * npp + i) * page_size + kofs
        dist = (past_len - kpos).astype(F32)
        s_parts.append(s - slope * dist)
    m_prev = m_ref[...]
    m_cur = s_parts[0].max(axis=1, keepdims=True)
    for i in range(1, npp):
        m_cur = jnp.maximum(m_cur, s_parts[i].max(axis=1, keepdims=True))
    m_new = jnp.maximum(m_prev, m_cur)
    alpha = jnp.exp(m_prev - m_new)
    l_new = alpha * l_ref[...]
    pv = None
    for i in range(npp):
        p = jnp.exp(s_parts[i] - m_new)
        l_new = l_new + jnp.sum(p, axis=1, keepdims=True)
        d = jnp.dot(p.astype(BF16), v_refs[i][0].astype(BF16), preferred_element_type=F32)
        pv = d if pv is None else pv + d
    acc_ref[...] = jnp.concatenate([alpha] * (width // LANES), axis=1) * acc_ref[...] + pv
    l_ref[...] = l_new
    m_ref[...] = m_new

    @pl.when(s_idx == n_pages // npp - 1)
    def _():
        lam = lam_ref[0]
        o1 = jnp.concatenate([acc_ref[2 * h:2 * h + 1, h * LANES:(h + 1) * LANES] for h in range(n_heads)], axis=0)
        o2 = jnp.concatenate([acc_ref[2 * h + 1:2 * h + 2, h * LANES:(h + 1) * LANES] for h in range(n_heads)],
                             axis=0)
        l1 = jnp.concatenate([l_ref[2 * h:2 * h + 1, :] for h in range(n_heads)], axis=0)
        l2 = jnp.concatenate([l_ref[2 * h + 1:2 * h + 2, :] for h in range(n_heads)], axis=0)
        o = o1 / l1 - lam * (o2 / l2)
        ms = jnp.mean(o * o, axis=-1, keepdims=True)
        o = o * lax.rsqrt(ms + EPS) * sw_ref[...] * out_scale
        o_ref[0] = (o * _silu(g_ref[0])).astype(o_ref.dtype)


def _attn_decode(page_table, lam, cache_k, cache_v, q, k_new, v_new, g, subln_w, slope_rows, out_scale):
    bsz, n_pages = page_table.shape
    n_phys, page_size, width = cache_k.shape
    n_heads = width // ATT_V_DIM
    n_hc = 2 * n_heads
    npp = PAGES_PER_STEP

    def page_spec(i):
        return pl.BlockSpec((1, page_size, width), lambda b, s, pt, lm: (pt[b, s * npp + i], 0, 0))

    rowv = pl.BlockSpec((1, 1, width), lambda b, s, *_: (b, 0, 0))
    headv = pl.BlockSpec((1, n_heads, ATT_V_DIM), lambda b, s, *_: (b, 0, 0))
    grid_spec = pltpu.PrefetchScalarGridSpec(
        num_scalar_prefetch=2,
        grid=(bsz, n_pages // npp),
        in_specs=[page_spec(i) for i in range(npp)] + [page_spec(i) for i in range(npp)] + [
            rowv, rowv, rowv, headv,
            pl.BlockSpec((1, ATT_V_DIM), lambda b, s, *_: (0, 0)),
            pl.BlockSpec((n_hc, LANES), lambda b, s, *_: (0, 0))],
        out_specs=headv,
        scratch_shapes=[pltpu.VMEM((n_hc, width), BF16), pltpu.VMEM((n_hc, LANES), F32),
                        pltpu.VMEM((n_hc, LANES), F32), pltpu.VMEM((n_hc, width), F32)],
    )
    out = pl.pallas_call(
        functools.partial(_attn_decode_kernel, n_pages=n_pages, past_len=n_pages * page_size,
                          page_size=page_size, out_scale=out_scale),
        grid_spec=grid_spec,
        out_shape=jax.ShapeDtypeStruct((bsz, n_heads, ATT_V_DIM), BF16),
        compiler_params=_cparams(("arbitrary", "arbitrary")),
        name="attn_decode",
    )(page_table, lam, *([cache_k] * npp), *([cache_v] * npp),
      q.reshape(bsz, 1, width), k_new.reshape(bsz, 1, width), v_new.reshape(bsz, 1, width),
      g.reshape(bsz, n_heads, ATT_V_DIM), subln_w, slope_rows)
    return out.reshape(bsz, width)


def _expand_matrix(n_rows, rep):
    r = lax.broadcasted_iota(jnp.int32, (n_rows, n_rows * rep), 0)
    c = lax.broadcasted_iota(jnp.int32, (n_rows, n_rows * rep), 1)
    return (c // rep == r).astype(BF16)


def kernel(x_prompt, x_sample, state_conv, state_ssm, cache_k, cache_v, page_table, c_prompt, c_sample,
           ada_w, ada_b, ln_w, ssm_w_in, ssm_conv_w, ssm_conv_b, ssm_dt_bias, ssm_a_log, ssm_d,
           ssm_norm_w, ssm_w_out, att_w_in, att_q_norm, att_k_norm, att_lambda, att_subln_w, att_w_out):
    batch, seq, d = x_prompt.shape
    dec_batch = x_sample.shape[0]
    n_ssm_heads = ssm_a_log.shape[1]
    d_inner = n_ssm_heads * SSM_HEAD_DIM
    conv_dim = ssm_conv_w.shape[2]
    att_width = att_w_out.shape[1]
    n_att_heads = att_width // ATT_V_DIM
    m = batch * seq
    tm = 256
    tiles_per_seq = seq // tm

    mod = _adaln(jnp.concatenate([c_prompt, c_sample], axis=0), ada_w, ada_b)

    def mods(i):
        parts = [mod[i, :, k * d:(k + 1) * d] for k in range(3)]
        return ([p[:batch].reshape(batch, 1, d) for p in parts],
                [p[batch:].reshape(1, dec_batch, d) for p in parts])

    xp = x_prompt.reshape(m, d)
    xs = x_sample.reshape(dec_batch, d)

    (sh_p, sc_p, gt_p), (sh_s, sc_s, gt_s) = mods(0)
    w_in = ssm_w_in[0].astype(BF16)
    wz, wx, wd = w_in[:, :d_inner], w_in[:, d_inner:d_inner + conv_dim], w_in[:, d_inner + conv_dim:]
    lnw0 = ln_w[0].reshape(1, d)
    conv_w, conv_b = ssm_conv_w[0], ssm_conv_b[0].reshape(1, conv_dim)
    dt_bias, a_log = ssm_dt_bias[0].reshape(1, -1), ssm_a_log[0].reshape(1, -1)
    d_skip_e = jnp.repeat(ssm_d[0], SSM_HEAD_DIM).reshape(1, d_inner)
    norm_w = ssm_norm_w[0].reshape(1, d_inner)
    w_out0 = ssm_w_out[0].astype(BF16)
    e64 = _expand_matrix(n_ssm_heads, SSM_HEAD_DIM)
    e128 = _expand_matrix(n_ssm_heads, D_STATE)

    z_p, xbc_p, dt_p = _inproj_ssm(xp, lnw0, sc_p, sh_p, wz, wx, wd, tm, tiles_per_seq)
    yn_p, ssm_p = _ssd_prompt(xbc_p, z_p, dt_p, conv_w, conv_b, dt_bias, a_log, d_skip_e, norm_w, e64, e128,
                              batch, seq)
    y1_p = _outproj(yn_p, w_out0, xp, gt_p, tm, tiles_per_seq)
    conv_p = xbc_p.reshape(batch, seq, conv_dim)[:, seq - (CONV_W - 1):, :]

    z_s, xbc_s, dt_s = _inproj_ssm(xs, lnw0, sc_s, sh_s, wz, wx, wd, dec_batch, 1)
    yn_s, conv_s, ssm_s = _ssd_step(xbc_s, z_s, dt_s, state_conv[0], state_ssm[0], conv_w, conv_b, dt_bias,
                                    a_log, d_skip_e, norm_w, e64, e128)
    y1_s = _outproj(yn_s.reshape(dec_batch, d_inner), w_out0, xs, gt_s, dec_batch, 1)

    (sh_p, sc_p, gt_p), (sh_s, sc_s, gt_s) = mods(1)
    lam_init = 0.8 - 0.6 * math.exp(-0.3 * 1)
    lmb = att_lambda[0].astype(F32)
    lam = (jnp.exp(jnp.sum(lmb[0] * lmb[1])) - jnp.exp(jnp.sum(lmb[2] * lmb[3])) + lam_init).reshape(1)
    slopes = jnp.exp2(-8.0 * jnp.arange(1, n_att_heads + 1, dtype=F32) / n_att_heads)
    wa = att_w_in[0].astype(BF16)
    wq, wk, wv, wg = (wa[:, i * att_width:(i + 1) * att_width] for i in range(4))
    lnw1 = ln_w[1].reshape(1, d)
    n_rep = att_width // ATT_HEAD_DIM
    qn = jnp.tile(att_q_norm[0], n_rep).reshape(1, att_width)
    kn = jnp.tile(att_k_norm[0], n_rep).reshape(1, att_width)
    gi = lax.broadcasted_iota(jnp.int32, (2 * LANES, 2 * LANES), 0) // ATT_HEAD_DIM
    gj = lax.broadcasted_iota(jnp.int32, (2 * LANES, 2 * LANES), 1) // ATT_HEAD_DIM
    gsum = (gi == gj).astype(BF16)
    subln_w = att_subln_w[0].reshape(1, ATT_V_DIM)
    w_out1 = att_w_out[0].astype(BF16)
    out_scale = 1.0 - lam_init

    q_p, k_p, v_p, g_p = _inproj_att(y1_p, lnw1, sc_p, sh_p, wq, wk, wv, wg, qn, kn, gsum, tm, tiles_per_seq)
    o_p = _attn_prompt(slopes, lam, q_p, k_p, v_p, g_p, subln_w, batch, seq, out_scale)
    y2_p = _outproj(o_p, w_out1, y1_p, gt_p, tm, tiles_per_seq)

    q_s, k_s, v_s, g_s = _inproj_att(y1_s, lnw1, sc_s, sh_s, wq, wk, wv, wg, qn, kn, gsum, dec_batch, 1)
    n_phys, page_size = cache_k.shape[1], cache_k.shape[2]
    slope_rows = jnp.broadcast_to(jnp.repeat(slopes, 2)[:, None], (2 * n_att_heads, LANES))
    o_s = _attn_decode(page_table, lam, cache_k[0].reshape(n_phys, page_size, att_width),
                       cache_v[0].reshape(n_phys, page_size, att_width), q_s, k_s, v_s, g_s, subln_w,
                       slope_rows, out_scale)
    y2_s = _outproj(o_s, w_out1, y1_s, gt_s, dec_batch, 1)

    return (y2_p.reshape(batch, seq, d),
            y2_s.reshape(dec_batch, 1, d),
            conv_p[None],
            ssm_p.reshape(1, batch, n_ssm_heads, SSM_HEAD_DIM, D_STATE),
            k_p.reshape(1, batch, seq, n_att_heads, 2, ATT_HEAD_DIM),
            v_p.reshape(1, batch, seq, n_att_heads, ATT_V_DIM),
            conv_s[None],
            ssm_s.reshape(1, dec_batch, n_ssm_heads, SSM_HEAD_DIM, D_STATE),
            k_s.reshape(1, dec_batch, 1, n_att_heads, 2, ATT_HEAD_DIM),
            v_s.reshape(1, dec_batch, 1, n_att_heads, ATT_V_DIM))
```

```python
import functools
import math

import jax
import jax.numpy as jnp
from jax import lax
from jax.experimental import pallas as pl
from jax.experimental.pallas import tpu as pltpu

F32 = jnp.float32
BF16 = jnp.bfloat16
EPS = 1e-6
NEG = -1e30

LANES = 128
SSD_CHUNK = 128
SSM_HEAD_DIM = 64
SSM_GROUPS = 8
D_STATE = 128
CONV_W = 4
ATT_HEAD_DIM = 64
ATT_V_DIM = 128
ATT_BLOCK = 512
PAGES_PER_STEP = 4
LOG2E = 1.4426950408889634
VMEM_LIMIT = 56 * 1024 * 1024


def _cparams(sem):
    return pltpu.CompilerParams(dimension_semantics=sem, vmem_limit_bytes=VMEM_LIMIT)


def _silu(x):
    return x / (1.0 + jnp.exp(-x))


def _softplus(x):
    return jnp.maximum(x, 0.0) + jnp.log(1.0 + jnp.exp(-jnp.abs(x)))


def _split_dot(v, e):
    hi = v.astype(BF16)
    lo = (v - hi.astype(F32)).astype(BF16)
    return (jnp.dot(hi, e, preferred_element_type=F32)
            + jnp.dot(lo, e, preferred_element_type=F32))


def _norm_mod(x, lnw, scale, shift):
    ms = jnp.mean(x * x, axis=-1, keepdims=True)
    y = x * lax.rsqrt(ms + EPS) * lnw
    return y * (1.0 + scale) + shift


def _adaln_kernel(c_ref, w_ref, b_ref, o_ref):
    s = _silu(c_ref[...])
    o_ref[0] = jnp.dot(s.astype(BF16), w_ref[0].astype(BF16), preferred_element_type=F32) + b_ref[0]


def _adaln(c_all, ada_w, ada_b):
    depth, d, d3 = ada_w.shape
    r = c_all.shape[0]
    return pl.pallas_call(
        _adaln_kernel,
        grid=(depth, d3 // d),
        in_specs=[pl.BlockSpec((r, d), lambda i, j: (0, 0)),
                  pl.BlockSpec((1, d, d), lambda i, j: (i, 0, j)),
                  pl.BlockSpec((1, 1, d), lambda i, j: (i, 0, j))],
        out_specs=pl.BlockSpec((1, r, d), lambda i, j: (i, 0, j)),
        out_shape=jax.ShapeDtypeStruct((depth, r, d3), F32),
        compiler_params=_cparams(("arbitrary", "arbitrary")),
        name="adaln",
    )(c_all, ada_w, ada_b.reshape(depth, 1, d3))


N_CHUNK = 512


def _inproj_ssm_kernel(x_ref, lnw_ref, sc_ref, sh_ref, wz_ref, wx_ref, wd_ref, z_ref, xbc_ref, dt_ref):
    h = _norm_mod(x_ref[...], lnw_ref[...], sc_ref[0], sh_ref[0]).astype(BF16)
    for w_ref, o_ref in ((wz_ref, z_ref), (wx_ref, xbc_ref)):
        for j in range(0, w_ref.shape[1], N_CHUNK):
            o_ref[:, j:j + N_CHUNK] = jnp.dot(h, w_ref[:, j:j + N_CHUNK], preferred_element_type=F32)
    dt_ref[...] = jnp.dot(h, wd_ref[...], preferred_element_type=F32)


def _mod_specs(mod_rows, d, tm, tiles_per_seq):
    if mod_rows == 1:
        return pl.BlockSpec((1, 1, d), lambda i: (i // tiles_per_seq, 0, 0))
    return pl.BlockSpec((1, tm, d), lambda i: (0, i, 0))


def _inproj_ssm(x, lnw, scale, shift, wz, wx, wd, tm, tiles_per_seq):
    m, d = x.shape
    full = lambda a: pl.BlockSpec(a.shape, lambda i: (0,) * a.ndim)
    row = lambda n: pl.BlockSpec((tm, n), lambda i: (i, 0))
    mod = _mod_specs(scale.shape[1], d, tm, tiles_per_seq)
    return pl.pallas_call(
        _inproj_ssm_kernel,
        grid=(m // tm,),
        in_specs=[row(d), full(lnw), mod, mod, full(wz), full(wx), full(wd)],
        out_specs=[row(wz.shape[1]), row(wx.shape[1]), row(wd.shape[1])],
        out_shape=[jax.ShapeDtypeStruct((m, wz.shape[1]), F32),
                   jax.ShapeDtypeStruct((m, wx.shape[1]), F32),
                   jax.ShapeDtypeStruct((m, wd.shape[1]), F32)],
        compiler_params=_cparams(("arbitrary",)),
        name="inproj_ssm",
    )(x, lnw, scale, shift, wz, wx, wd)


def _inproj_att_kernel(x_ref, lnw_ref, sc_ref, sh_ref, wq_ref, wkt_ref, wv_ref, wg_ref, qn_ref, kn_ref,
                       gsum_ref, q_ref, kt_ref, v_ref, g_ref, *, q_scale):
    h = _norm_mod(x_ref[...], lnw_ref[...], sc_ref[0], sh_ref[0]).astype(BF16)
    tm = h.shape[0]
    gsum = gsum_ref[...]
    width = gsum.shape[0]
    inv_hd = 1.0 / ATT_HEAD_DIM
    for j in range(0, wq_ref.shape[1], N_CHUNK):
        qc = jnp.dot(h, wq_ref[:, j:j + N_CHUNK], preferred_element_type=F32)
        for i in range(0, N_CHUNK, width):
            t = qc[:, i:i + width]
            ms = _split_dot(t * t, gsum) * inv_hd
            qn = t * lax.rsqrt(ms + EPS) * qn_ref[:, j + i:j + i + width]
            q_ref[:, j + i:j + i + width] = (qn * q_scale).astype(q_ref.dtype)
        v_ref[:, j:j + N_CHUNK] = jnp.dot(h, wv_ref[:, j:j + N_CHUNK], preferred_element_type=F32)
        g_ref[:, j:j + N_CHUNK] = jnp.dot(h, wg_ref[:, j:j + N_CHUNK], preferred_element_type=F32)
        kc = lax.dot_general(wkt_ref[j:j + N_CHUNK, :], h, (((1,), (1,)), ((), ())), preferred_element_type=F32)
        k3 = kc.reshape(N_CHUNK // ATT_HEAD_DIM, ATT_HEAD_DIM, tm)
        ms = jnp.mean(k3 * k3, axis=1, keepdims=True)
        kt_ref[0, j:j + N_CHUNK, :] = (k3 * lax.rsqrt(ms + EPS) * kn_ref[...][None]).reshape(N_CHUNK, tm)


def _inproj_att(x, lnw, scale, shift, wq, wkt, wv, wg, qn, kn_rep, gsum, tm, tiles_per_seq):
    m, d = x.shape
    n = wq.shape[1]
    seq = tm * tiles_per_seq
    full = lambda a: pl.BlockSpec(a.shape, lambda i: (0,) * a.ndim)
    row = lambda c: pl.BlockSpec((tm, c), lambda i: (i, 0))
    mod = _mod_specs(scale.shape[1], d, tm, tiles_per_seq)
    return pl.pallas_call(
        functools.partial(_inproj_att_kernel, q_scale=ATT_HEAD_DIM ** -0.5 * LOG2E),
        grid=(m // tm,),
        in_specs=[row(d), full(lnw), mod, mod, full(wq), full(wkt), full(wv), full(wg), full(qn), full(kn_rep),
                  full(gsum)],
        out_specs=[row(n), pl.BlockSpec((1, n, tm), lambda i: (i // tiles_per_seq, 0, i % tiles_per_seq)),
                   row(n), row(n)],
        out_shape=[jax.ShapeDtypeStruct((m, n), BF16), jax.ShapeDtypeStruct((m // seq, n, seq), F32),
                   jax.ShapeDtypeStruct((m, n), F32), jax.ShapeDtypeStruct((m, n), F32)],
        compiler_params=_cparams(("arbitrary",)),
        name="inproj_att",
    )(x, lnw, scale, shift, wq, wkt, wv, wg, qn, kn_rep, gsum)


def _outproj_kernel(a_ref, w_ref, x_ref, gate_ref, o_ref):
    y = jnp.dot(a_ref[...], w_ref[...], preferred_element_type=F32)
    o_ref[...] = x_ref[...] + gate_ref[0] * y


def _outproj(a, w, xres, gate, tm, tiles_per_seq):
    m, k = a.shape
    d = w.shape[1]
    return pl.pallas_call(
        _outproj_kernel,
        grid=(m // tm,),
        in_specs=[pl.BlockSpec((tm, k), lambda i: (i, 0)),
                  pl.BlockSpec((k, d), lambda i: (0, 0)),
                  pl.BlockSpec((tm, d), lambda i: (i, 0)),
                  _mod_specs(gate.shape[1], d, tm, tiles_per_seq)],
        out_specs=pl.BlockSpec((tm, d), lambda i: (i, 0)),
        out_shape=jax.ShapeDtypeStruct((m, d), F32),
        compiler_params=_cparams(("arbitrary",)),
        name="outproj",
    )(a, w, xres, gate)


def _ssd_prompt_kernel(xbc_ref, z_ref, dt_ref, cw_ref, cb_ref, dtb_ref, alog_ref, dsk_ref, nw_ref,
                       e64_ref, e128_ref, y_ref, st_ref, xp_ref, xc_ref, *, d_inner):
    q = SSD_CHUNK
    conv_dim = xbc_ref.shape[1]
    n_heads = dt_ref.shape[1]
    hpg = n_heads // SSM_GROUPS
    gw = hpg * SSM_HEAD_DIM
    c_idx = pl.program_id(1)

    @pl.when(c_idx == 0)
    def _():
        xp_ref[0:8, :] = jnp.zeros((8, conv_dim), F32)
        st_ref[...] = jnp.zeros(st_ref.shape, F32)

    xp_ref[8:8 + q, :] = xbc_ref[...]
    for j in range(0, conv_dim, N_CHUNK):
        cs_ = slice(j, j + N_CHUNK)
        acc = cb_ref[:, cs_] + cw_ref[3:4, cs_] * xp_ref[8:8 + q, cs_]
        for k in range(CONV_W - 1):
            acc = acc + cw_ref[k:k + 1, cs_] * xp_ref[5 + k:5 + k + q, cs_]
        xc_ref[:, cs_] = _silu(acc)
    xp_ref[0:8, :] = xp_ref[q:q + 8, :]

    dt = _softplus(dt_ref[...] + dtb_ref[...])
    a = -jnp.exp(alog_ref[...])
    dta = dt * a
    ti = lax.broadcasted_iota(jnp.int32, (q, q), 0)
    si = lax.broadcasted_iota(jnp.int32, (q, q), 1)
    causal = si <= ti
    tril = jnp.where(causal, 1.0, 0.0).astype(F32)
    cs = jnp.dot(tril, dta, precision=lax.Precision.HIGHEST, preferred_element_type=F32)
    cs_t = lax.dot_general(dta, tril, (((0,), (1,)), ((), ())), precision=lax.Precision.HIGHEST,
                           preferred_element_type=F32)
    cs_last = cs[q - 1:q, :]
    e64 = e64_ref[...]
    dt_e = _split_dot(dt, e64)
    ecs_e = _split_dot(jnp.exp(cs), e64)
    toend_e = _split_dot(jnp.exp(cs_last - cs), e64)
    dec_rows = _split_dot(jnp.broadcast_to(jnp.exp(cs_last), (8, n_heads)), e128_ref[...])

    lane = lax.broadcasted_iota(jnp.int32, (q, LANES), 1)
    left = lane < SSM_HEAD_DIM
    b_off = d_inner
    c_off = d_inner + SSM_GROUPS * D_STATE
    for g in range(SSM_GROUPS):
        gs = slice(g * gw, (g + 1) * gw)
        xs = xc_ref[:, gs]
        bg = xc_ref[:, b_off + g * D_STATE:b_off + (g + 1) * D_STATE].astype(BF16)
        cg = xc_ref[:, c_off + g * D_STATE:c_off + (g + 1) * D_STATE].astype(BF16)
        xdt = xs * dt_e[:, gs]
        cb = lax.dot_general(cg, bg, (((1,), (1,)), ((), ())), preferred_element_type=F32)
        y_parts = []
        for pr in range(hpg // 2):
            ms = []
            for hh in range(2):
                h = g * hpg + pr * 2 + hh
                seg = cs[:, h:h + 1] - cs_t[h:h + 1, :]
                dec = jnp.exp(jnp.where(causal, seg, NEG))
                ms.append((cb * dec).astype(BF16))
            xpair = xdt[:, pr * LANES:(pr + 1) * LANES]
            rhs = jnp.concatenate([jnp.where(left, xpair, 0.0), jnp.where(left, 0.0, xpair)], axis=0)
            y_parts.append(jnp.dot(jnp.concatenate(ms, axis=1), rhs.astype(BF16), preferred_element_type=F32))
        y = jnp.concatenate(y_parts, axis=1)
        st = st_ref[0, g * gw:(g + 1) * gw, :]
        y_off = lax.dot_general(cg, st.astype(BF16), (((1,), (1,)), ((), ())), preferred_element_type=F32)
        y = y + y_off * ecs_e[:, gs] + xs * dsk_ref[:, gs]
        upd = lax.dot_general((xdt * toend_e[:, gs]).astype(BF16), bg, (((0,), (0,)), ((), ())),
                              preferred_element_type=F32)
        for hh in range(hpg):
            h = g * hpg + hh
            rs = slice(hh * SSM_HEAD_DIM, (hh + 1) * SSM_HEAD_DIM)
            st_ref[0, g * gw + hh * SSM_HEAD_DIM:g * gw + (hh + 1) * SSM_HEAD_DIM, :] = (
                st[rs, :] * dec_rows[0:1, h * D_STATE:(h + 1) * D_STATE] + upd[rs, :])
        y = y * _silu(z_ref[:, gs])
        ms_ = jnp.mean(y * y, axis=-1, keepdims=True)
        y_ref[:, gs] = (y * lax.rsqrt(ms_ + EPS) * nw_ref[:, gs]).astype(y_ref.dtype)


def _ssd_prompt(xbc, z, dt, conv_w, conv_b, dt_bias, a_log, d_skip_e, norm_w, e64, e128, batch, seq):
    m, conv_dim = xbc.shape
    d_inner = z.shape[1]
    n_heads = dt.shape[1]
    nc = seq // SSD_CHUNK
    full = lambda a: pl.BlockSpec(a.shape, lambda b, c: (0,) * a.ndim)
    row = lambda n: pl.BlockSpec((SSD_CHUNK, n), lambda b, c: (b * nc + c, 0))
    return pl.pallas_call(
        functools.partial(_ssd_prompt_kernel, d_inner=d_inner),
        grid=(batch, nc),
        in_specs=[row(conv_dim), row(d_inner), row(n_heads), full(conv_w), full(conv_b), full(dt_bias),
                  full(a_log), full(d_skip_e), full(norm_w), full(e64), full(e128)],
        out_specs=[row(d_inner),
                   pl.BlockSpec((1, n_heads * SSM_HEAD_DIM, D_STATE), lambda b, c: (b, 0, 0))],
        out_shape=[jax.ShapeDtypeStruct((m, d_inner), BF16),
                   jax.ShapeDtypeStruct((batch, n_heads * SSM_HEAD_DIM, D_STATE), F32)],
        scratch_shapes=[pltpu.VMEM((SSD_CHUNK + 8, conv_dim), F32), pltpu.VMEM((SSD_CHUNK, conv_dim), F32)],
        compiler_params=_cparams(("arbitrary", "arbitrary")),
        name="ssd_prompt",
    )(xbc, z, dt, conv_w, conv_b, dt_bias, a_log, d_skip_e, norm_w, e64, e128)


def _ssd_step_kernel(xbc_ref, z_ref, dt_ref, cst_ref, st_ref, cw_ref, cb_ref, dtb_ref, alog_ref, dsk_ref,
                     nw_ref, e64_ref, e128_ref, y_ref, cst_out_ref, st_out_ref, *, d_inner):
    n_heads = dt_ref.shape[2]
    hpg = n_heads // SSM_GROUPS
    gw = hpg * SSM_HEAD_DIM
    x_new = xbc_ref[0]
    prev = cst_ref[0]
    acc = cb_ref[...] + cw_ref[CONV_W - 1:CONV_W, :] * x_new
    for k in range(CONV_W - 1):
        acc = acc + cw_ref[k:k + 1, :] * prev[k:k + 1, :]
    xc = _silu(acc)
    cst_out_ref[0, 0:CONV_W - 2, :] = prev[1:CONV_W - 1, :]
    cst_out_ref[0, CONV_W - 2:CONV_W - 1, :] = x_new

    dt = _softplus(dt_ref[0] + dtb_ref[...])
    a = -jnp.exp(alog_ref[...])
    dec = jnp.exp(dt * a)
    dt8 = jnp.broadcast_to(dt, (8, n_heads))
    dt_e = _split_dot(dt8, e64_ref[...])[0:1, :]
    dec_rows = _split_dot(jnp.broadcast_to(dec, (8, n_heads)), e128_ref[...])
    row0 = lax.broadcasted_iota(jnp.int32, (8, D_STATE), 0) == 0
    b_off = d_inner
    c_off = d_inner + SSM_GROUPS * D_STATE
    for g in range(SSM_GROUPS):
        gs = slice(g * gw, (g + 1) * gw)
        xs = xc[:, gs]
        xdt8 = jnp.broadcast_to(xs * dt_e[:, gs], (8, gw)).astype(BF16)
        bg = xc[:, b_off + g * D_STATE:b_off + (g + 1) * D_STATE]
        cg = xc[:, c_off + g * D_STATE:c_off + (g + 1) * D_STATE]
        b8 = jnp.where(row0, jnp.broadcast_to(bg, (8, D_STATE)), 0.0).astype(BF16)
        c8 = jnp.broadcast_to(cg, (8, D_STATE)).astype(BF16)
        upd = lax.dot_general(xdt8, b8, (((0,), (0,)), ((), ())), preferred_element_type=F32)
        news = []
        for hh in range(hpg):
            h = g * hpg + hh
            rs = slice(g * gw + hh * SSM_HEAD_DIM, g * gw + (hh + 1) * SSM_HEAD_DIM)
            new = (st_ref[0, rs, :] * dec_rows[0:1, h * D_STATE:(h + 1) * D_STATE]
                   + upd[hh * SSM_HEAD_DIM:(hh + 1) * SSM_HEAD_DIM, :])
            st_out_ref[0, rs, :] = new
            news.append(new)
        st_new = jnp.concatenate(news, axis=0).astype(BF16)
        y = lax.dot_general(c8, st_new, (((1,), (1,)), ((), ())), preferred_element_type=F32)[0:1, :]
        y = y + xs * dsk_ref[:, gs]
        y = y * _silu(z_ref[0, :, gs])
        ms_ = jnp.mean(y * y, axis=-1, keepdims=True)
        y_ref[0, :, gs] = (y * lax.rsqrt(ms_ + EPS) * nw_ref[:, gs]).astype(y_ref.dtype)


def _ssd_step(xbc, z, dt, conv_state, ssm_state, conv_w, conv_b, dt_bias, a_log, d_skip_e, norm_w, e64, e128):
    bsz, conv_dim = xbc.shape
    d_inner = z.shape[1]
    n_heads = dt.shape[1]
    rows = n_heads * SSM_HEAD_DIM
    full = lambda a: pl.BlockSpec(a.shape, lambda b: (0,) * a.ndim)
    per_b = lambda *s: pl.BlockSpec((1,) + s, lambda b: (b,) + (0,) * len(s))
    return pl.pallas_call(
        functools.partial(_ssd_step_kernel, d_inner=d_inner),
        grid=(bsz,),
        in_specs=[per_b(1, conv_dim), per_b(1, d_inner), per_b(1, n_heads), per_b(CONV_W - 1, conv_dim),
                  per_b(rows, D_STATE), full(conv_w), full(conv_b), full(dt_bias), full(a_log),
                  full(d_skip_e), full(norm_w), full(e64), full(e128)],
        out_specs=[per_b(1, d_inner), per_b(CONV_W - 1, conv_dim), per_b(rows, D_STATE)],
        out_shape=[jax.ShapeDtypeStruct((bsz, 1, d_inner), BF16),
                   jax.ShapeDtypeStruct((bsz, CONV_W - 1, conv_dim), F32),
                   jax.ShapeDtypeStruct((bsz, rows, D_STATE), F32)],
        compiler_params=_cparams(("arbitrary",)),
        name="ssd_step",
    )(xbc.reshape(bsz, 1, conv_dim), z.reshape(bsz, 1, d_inner), dt.reshape(bsz, 1, n_heads), conv_state,
      ssm_state.reshape(bsz, rows, D_STATE), conv_w, conv_b, dt_bias, a_log, d_skip_e, norm_w, e64, e128)


def _attn_prompt_kernel(slopes_ref, lam_ref, q_ref, k_ref, v_ref, g_ref, sw_ref, o_ref,
                        kb_ref, vb_ref, qs_ref, m_ref, a_ref, *, out_scale):
    t = ATT_BLOCK
    seq = q_ref.shape[0]
    nblk = seq // t
    slope = slopes_ref[pl.program_id(1)] * LOG2E
    lam = lam_ref[0]
    for j in range(nblk):
        kb_ref[j] = k_ref[0, 0, :, :, j * t:(j + 1) * t].reshape(2 * ATT_HEAD_DIM, t).astype(BF16)
    vb_ref[:, 0:ATT_V_DIM] = v_ref[...].astype(BF16)
    vb_ref[:, ATT_V_DIM:2 * ATT_V_DIM] = jnp.ones((seq, ATT_V_DIM), BF16)
    first = lax.broadcasted_iota(jnp.int32, (t, LANES), 1) < ATT_HEAD_DIM
    kidx = lax.broadcasted_iota(jnp.int32, (1, t), 1).astype(F32)

    def q_block(qi, carry):
        q0 = pl.multiple_of(qi * t, t)
        qp = q_ref[pl.ds(q0, t), :].astype(F32)
        qs_ref[0:t, :] = jnp.where(first, qp, 0.0).astype(BF16)
        qs_ref[t:2 * t, :] = jnp.where(first, 0.0, qp).astype(BF16)

        ri = lax.broadcasted_iota(jnp.int32, (2 * t, t), 0)
        ci = lax.broadcasted_iota(jnp.int32, (2 * t, t), 1)
        causal = ci <= jnp.where(ri >= t, ri - t, ri)
        s = jnp.dot(qs_ref[...], kb_ref[qi], preferred_element_type=F32) + slope * kidx
        s = jnp.where(causal, s, NEG)
        m = jnp.max(s, axis=1, keepdims=True)
        p = jnp.exp2(s - m)
        m_ref[...] = jnp.broadcast_to(m, (2 * t, LANES))
        a_ref[...] = jnp.dot(p.astype(BF16), vb_ref[pl.ds(q0, t), :], preferred_element_type=F32)

        def kv_block(j, carry2):
            k0 = pl.multiple_of(j * t, t)
            bias = slope * (kidx + (k0 - q0).astype(F32))
            s = jnp.dot(qs_ref[...], kb_ref[j], preferred_element_type=F32) + bias
            m_prev = m_ref[...]
            m_new = jnp.maximum(m_prev, jnp.max(s, axis=1, keepdims=True))
            alpha = jnp.exp2(m_prev - m_new)
            p = jnp.exp2(s - jnp.concatenate([m_new] * (t // LANES), axis=1))
            a_ref[...] = (jnp.concatenate([alpha, alpha], axis=1) * a_ref[...]
                          + jnp.dot(p.astype(BF16), vb_ref[pl.ds(k0, t), :], preferred_element_type=F32))
            m_ref[...] = m_new
            return carry2

        lax.fori_loop(0, qi, kv_block, 0)

        o = (a_ref[0:t, 0:ATT_V_DIM] / a_ref[0:t, ATT_V_DIM:]
             - lam * (a_ref[t:2 * t, 0:ATT_V_DIM] / a_ref[t:2 * t, ATT_V_DIM:]))
        ms = jnp.mean(o * o, axis=-1, keepdims=True)
        o = o * lax.rsqrt(ms + EPS) * sw_ref[...] * out_scale
        o_ref[pl.ds(q0, t), :] = (o * _silu(g_ref[pl.ds(q0, t), :])).astype(o_ref.dtype)
        return carry

    lax.fori_loop(0, nblk, q_block, 0)


def _attn_prompt(slopes, lam, q, kt, v, g, subln_w, batch, seq, out_scale):
    m, width = q.shape
    n_heads = width // ATT_V_DIM
    t = ATT_BLOCK
    blk = pl.BlockSpec((seq, ATT_V_DIM), lambda b, h, *_: (b, h))
    grid_spec = pltpu.PrefetchScalarGridSpec(
        num_scalar_prefetch=2,
        grid=(batch, n_heads),
        in_specs=[blk,
                  pl.BlockSpec((1, 1, 2, ATT_HEAD_DIM, seq), lambda b, h, *_: (b, h, 0, 0, 0)),
                  blk, blk, pl.BlockSpec((1, ATT_V_DIM), lambda b, h, *_: (0, 0))],
        out_specs=blk,
        scratch_shapes=[pltpu.VMEM((seq // t, 2 * ATT_HEAD_DIM, t), BF16), pltpu.VMEM((seq, 2 * ATT_V_DIM), BF16),
                        pltpu.VMEM((2 * t, LANES), BF16),
                        pltpu.VMEM((2 * t, LANES), F32), pltpu.VMEM((2 * t, 2 * ATT_V_DIM), F32)],
    )
    return pl.pallas_call(
        functools.partial(_attn_prompt_kernel, out_scale=out_scale),
        grid_spec=grid_spec,
        out_shape=jax.ShapeDtypeStruct((m, width), BF16),
        compiler_params=_cparams(("arbitrary", "arbitrary")),
        name="attn_prompt",
    )(slopes, lam, q, kt, v, g, subln_w)


def _attn_decode_kernel(pt_ref, lam_ref, *refs, n_pages, past_len, page_size, out_scale):
    del pt_ref
    npp = PAGES_PER_STEP
    k_refs = refs[:npp]
    v_refs = refs[npp:2 * npp]
    (q_ref, kn_ref, vn_ref, g_ref, sw_ref, slope_ref, e_ref, o_ref,
     qb_ref, mk_ref, m_ref, l_ref, acc_ref) = refs[2 * npp:]
    s_idx = pl.program_id(1)
    n_hc, width = qb_ref.shape
    n_heads = n_hc // 2

    @pl.when(s_idx == 0)
    def _():
        r = lax.broadcasted_iota(jnp.int32, (n_hc, width), 0)
        ln = lax.broadcasted_iota(jnp.int32, (n_hc, width), 1)
        hc = 2 * (r % n_heads) + r // n_heads
        sel = ln // ATT_HEAD_DIM == hc
        mk_ref[...] = jnp.where(ln % n_heads == r % n_heads, 1.0, 0.0).astype(BF16)
        qrow = jnp.broadcast_to(q_ref[0].astype(F32), (n_hc, width))
        qblk = jnp.where(sel, qrow, 0.0)
        qb_ref[...] = qblk.astype(BF16)
        s_new = jnp.sum(qblk * kn_ref[0], axis=1, keepdims=True)
        m_ref[...] = jnp.broadcast_to(s_new, (n_hc, LANES))
        l_ref[...] = jnp.ones((n_hc, LANES), F32)
        vn = vn_ref[0]
        acc_ref[...] = jnp.concatenate([vn, vn], axis=0)

    qblk = qb_ref[...]
    slope = slope_ref[...]
    kofs = lax.broadcasted_iota(jnp.int32, (1, page_size), 1)
    s_parts = []
    for i in range(npp):
        kt = k_refs[i][0].reshape(width, page_size).astype(BF16)
        s = jnp.dot(qblk, kt, preferred_element_type=F32)
        kpos = (s_idx * npp + i) * page_size + kofs
        dist = (past_len - kpos).astype(F32)
        s_parts.append(s - slope * dist)
    m_prev = m_ref[...]
    m_cur = s_parts[0].max(axis=1, keepdims=True)
    for i in range(1, npp):
        m_cur = jnp.maximum(m_cur, s_parts[i].max(axis=1, keepdims=True))
    m_new = jnp.maximum(m_prev, m_cur)
    alpha = jnp.exp2(m_prev - m_new)
    l_new = alpha * l_ref[...]
    pv = None
    for i in range(npp):
        p = jnp.exp2(s_parts[i] - m_new)
        l_new = l_new + jnp.sum(p, axis=1, keepdims=True)
        pe = jnp.dot(p.astype(BF16), e_ref[...], preferred_element_type=F32).astype(BF16) * mk_ref[...]
        v2 = v_refs[i][0].reshape(page_size * n_heads, ATT_V_DIM).astype(BF16)
        d = jnp.dot(pe, v2, preferred_element_type=F32)
        pv = d if pv is None else pv + d
    acc_ref[...] = alpha * acc_ref[...] + pv
    l_ref[...] = l_new
    m_ref[...] = m_new

    @pl.when(s_idx == n_pages // npp - 1)
    def _():
        lam = lam_ref[0]
        o = (acc_ref[0:n_heads, :] / l_ref[0:n_heads, :]
             - lam * (acc_ref[n_heads:n_hc, :] / l_ref[n_heads:n_hc, :]))
        ms = jnp.mean(o * o, axis=-1, keepdims=True)
        o = o * lax.rsqrt(ms + EPS) * sw_ref[...] * out_scale
        o_ref[0] = (o * _silu(g_ref[0])).astype(o_ref.dtype)


def _attn_decode(page_table, lam, kc, vc, q, k_new, v_new, g, subln_w, slope_rows, e_tok, out_scale):
    bsz, n_pages = page_table.shape
    n_phys, n_heads, _, _, page_size = kc.shape
    width = n_heads * ATT_V_DIM
    n_hc = 2 * n_heads
    npp = PAGES_PER_STEP

    def kspec(i):
        return pl.BlockSpec((1, n_heads, 2, ATT_HEAD_DIM, page_size),
                            lambda b, s, pt, lm: (pt[b, s * npp + i], 0, 0, 0, 0))

    def vspec(i):
        return pl.BlockSpec((1, page_size, n_heads, ATT_V_DIM), lambda b, s, pt, lm: (pt[b, s * npp + i], 0, 0, 0))

    rowv = pl.BlockSpec((1, 1, width), lambda b, s, *_: (b, 0, 0))
    headv = pl.BlockSpec((1, n_heads, ATT_V_DIM), lambda b, s, *_: (b, 0, 0))
    grid_spec = pltpu.PrefetchScalarGridSpec(
        num_scalar_prefetch=2,
        grid=(bsz, n_pages // npp),
        in_specs=[kspec(i) for i in range(npp)] + [vspec(i) for i in range(npp)] + [
            rowv, rowv, headv, headv,
            pl.BlockSpec((1, ATT_V_DIM), lambda b, s, *_: (0, 0)),
            pl.BlockSpec((n_hc, LANES), lambda b, s, *_: (0, 0)),
            pl.BlockSpec((page_size, page_size * n_heads), lambda b, s, *_: (0, 0))],
        out_specs=headv,
        scratch_shapes=[pltpu.VMEM((n_hc, width), BF16), pltpu.VMEM((n_hc, page_size * n_heads), BF16),
                        pltpu.VMEM((n_hc, LANES), F32), pltpu.VMEM((n_hc, LANES), F32),
                        pltpu.VMEM((n_hc, ATT_V_DIM), F32)],
    )
    out = pl.pallas_call(
        functools.partial(_attn_decode_kernel, n_pages=n_pages, past_len=n_pages * page_size,
                          page_size=page_size, out_scale=out_scale),
        grid_spec=grid_spec,
        out_shape=jax.ShapeDtypeStruct((bsz, n_heads, ATT_V_DIM), BF16),
        compiler_params=_cparams(("arbitrary", "arbitrary")),
        name="attn_decode",
    )(page_table, lam, *([kc] * npp), *([vc] * npp),
      q.reshape(bsz, 1, width), k_new.reshape(bsz, 1, width), v_new.reshape(bsz, n_heads, ATT_V_DIM),
      g.reshape(bsz, n_heads, ATT_V_DIM), subln_w, slope_rows, e_tok)
    return out.reshape(bsz, width)


def _expand_matrix(n_rows, rep):
    r = lax.broadcasted_iota(jnp.int32, (n_rows, n_rows * rep), 0)
    c = lax.broadcasted_iota(jnp.int32, (n_rows, n_rows * rep), 1)
    return (c // rep == r).astype(BF16)


def kernel(x_prompt, x_sample, state_conv, state_ssm, cache_k, cache_v, page_table, c_prompt, c_sample,
           ada_w, ada_b, ln_w, ssm_w_in, ssm_conv_w, ssm_conv_b, ssm_dt_bias, ssm_a_log, ssm_d,
           ssm_norm_w, ssm_w_out, att_w_in, att_q_norm, att_k_norm, att_lambda, att_subln_w, att_w_out):
    batch, seq, d = x_prompt.shape
    dec_batch = x_sample.shape[0]
    n_ssm_heads = ssm_a_log.shape[1]
    d_inner = n_ssm_heads * SSM_HEAD_DIM
    conv_dim = ssm_conv_w.shape[2]
    att_width = att_w_out.shape[1]
    n_att_heads = att_width // ATT_V_DIM
    m = batch * seq
    tm = 256
    tiles_per_seq = seq // tm

    mod = _adaln(jnp.concatenate([c_prompt, c_sample], axis=0), ada_w, ada_b)

    def mods(i):
        parts = [mod[i, :, k * d:(k + 1) * d] for k in range(3)]
        return ([p[:batch].reshape(batch, 1, d) for p in parts],
                [p[batch:].reshape(1, dec_batch, d) for p in parts])

    xp = x_prompt.reshape(m, d)
    xs = x_sample.reshape(dec_batch, d)

    (sh_p, sc_p, gt_p), (sh_s, sc_s, gt_s) = mods(0)
    w_in = ssm_w_in[0].astype(BF16)
    wz, wx, wd = w_in[:, :d_inner], w_in[:, d_inner:d_inner + conv_dim], w_in[:, d_inner + conv_dim:]
    lnw0 = ln_w[0].reshape(1, d)
    conv_w, conv_b = ssm_conv_w[0], ssm_conv_b[0].reshape(1, conv_dim)
    dt_bias, a_log = ssm_dt_bias[0].reshape(1, -1), ssm_a_log[0].reshape(1, -1)
    d_skip_e = jnp.repeat(ssm_d[0], SSM_HEAD_DIM).reshape(1, d_inner)
    norm_w = ssm_norm_w[0].reshape(1, d_inner)
    w_out0 = ssm_w_out[0].astype(BF16)
    e64 = _expand_matrix(n_ssm_heads, SSM_HEAD_DIM)
    e128 = _expand_matrix(n_ssm_heads, D_STATE)

    z_p, xbc_p, dt_p = _inproj_ssm(xp, lnw0, sc_p, sh_p, wz, wx, wd, tm, tiles_per_seq)
    yn_p, ssm_p = _ssd_prompt(xbc_p, z_p, dt_p, conv_w, conv_b, dt_bias, a_log, d_skip_e, norm_w, e64, e128,
                              batch, seq)
    y1_p = _outproj(yn_p, w_out0, xp, gt_p, tm, tiles_per_seq)
    conv_p = xbc_p.reshape(batch, seq, conv_dim)[:, seq - (CONV_W - 1):, :]

    z_s, xbc_s, dt_s = _inproj_ssm(xs, lnw0, sc_s, sh_s, wz, wx, wd, dec_batch, 1)
    yn_s, conv_s, ssm_s = _ssd_step(xbc_s, z_s, dt_s, state_conv[0], state_ssm[0], conv_w, conv_b, dt_bias,
                                    a_log, d_skip_e, norm_w, e64, e128)
    y1_s = _outproj(yn_s.reshape(dec_batch, d_inner), w_out0, xs, gt_s, dec_batch, 1)

    (sh_p, sc_p, gt_p), (sh_s, sc_s, gt_s) = mods(1)
    lam_init = 0.8 - 0.6 * math.exp(-0.3 * 1)
    lmb = att_lambda[0].astype(F32)
    lam = (jnp.exp(jnp.sum(lmb[0] * lmb[1])) - jnp.exp(jnp.sum(lmb[2] * lmb[3])) + lam_init).reshape(1)
    slopes = jnp.exp2(-8.0 * jnp.arange(1, n_att_heads + 1, dtype=F32) / n_att_heads)
    wa = att_w_in[0].astype(BF16)
    wq, wk, wv, wg = (wa[:, i * att_width:(i + 1) * att_width] for i in range(4))
    wkt = wk.T
    lnw1 = ln_w[1].reshape(1, d)
    n_rep = att_width // ATT_HEAD_DIM
    qn = jnp.tile(att_q_norm[0], n_rep).reshape(1, att_width)
    gi = lax.broadcasted_iota(jnp.int32, (2 * LANES, 2 * LANES), 0) // ATT_HEAD_DIM
    gj = lax.broadcasted_iota(jnp.int32, (2 * LANES, 2 * LANES), 1) // ATT_HEAD_DIM
    gsum = (gi == gj).astype(BF16)
    subln_w = att_subln_w[0].reshape(1, ATT_V_DIM)
    w_out1 = att_w_out[0].astype(BF16)
    out_scale = 1.0 - lam_init

    def kn_rep(cols):
        return jnp.broadcast_to(att_k_norm[0][:, None], (ATT_HEAD_DIM, cols))

    q_p, kt_p, v_p, g_p = _inproj_att(y1_p, lnw1, sc_p, sh_p, wq, wkt, wv, wg, qn, kn_rep(tm), gsum, tm,
                                      tiles_per_seq)
    kt_p = kt_p.reshape(batch, n_att_heads, 2, ATT_HEAD_DIM, seq)
    o_p = _attn_prompt(slopes, lam, q_p, kt_p, v_p, g_p, subln_w, batch, seq, out_scale)
    y2_p = _outproj(o_p, w_out1, y1_p, gt_p, tm, tiles_per_seq)

    q_s, kt_s, v_s, g_s = _inproj_att(y1_s, lnw1, sc_s, sh_s, wq, wkt, wv, wg, qn, kn_rep(dec_batch), gsum,
                                      dec_batch, 1)
    k_s = kt_s[0].T
    page_size = cache_k.shape[2]
    n_hc = 2 * n_att_heads
    slope_rows = jnp.broadcast_to((jnp.tile(slopes, 2) * LOG2E)[:, None], (n_hc, LANES))
    ei = lax.broadcasted_iota(jnp.int32, (page_size, page_size * n_att_heads), 0)
    ej = lax.broadcasted_iota(jnp.int32, (page_size, page_size * n_att_heads), 1) // n_att_heads
    e_tok = (ei == ej).astype(BF16)
    kc = jnp.transpose(cache_k[0], (0, 2, 3, 4, 1))
    o_s = _attn_decode(page_table, lam, kc, cache_v[0], q_s, k_s, v_s, g_s, subln_w, slope_rows, e_tok, out_scale)
    y2_s = _outproj(o_s, w_out1, y1_s, gt_s, dec_batch, 1)
    k_p = jnp.transpose(kt_p, (0, 4, 1, 2, 3))

    return (y2_p.reshape(batch, seq, d),
            y2_s.reshape(dec_batch, 1, d),
            conv_p[None],
            ssm_p.reshape(1, batch, n_ssm_heads, SSM_HEAD_DIM, D_STATE),
            k_p[None],
            v_p.reshape(1, batch, seq, n_att_heads, ATT_V_DIM),
            conv_s[None],
            ssm_s.reshape(1, dec_batch, n_ssm_heads, SSM_HEAD_DIM, D_STATE),
            k_s.reshape(1, dec_batch, 1, n_att_heads, 2, ATT_HEAD_DIM),
            v_s.reshape(1, dec_batch, 1, n_att_heads, ATT_V_DIM))
```

```python
import functools
import math

import jax
import jax.numpy as jnp
from jax import lax
from jax.experimental import pallas as pl
from jax.experimental.pallas import tpu as pltpu

F32 = jnp.float32
BF16 = jnp.bfloat16
EPS = 1e-6
NEG = -1e30

LANES = 128
SSD_CHUNK = 128
SSM_HEAD_DIM = 64
SSM_GROUPS = 8
D_STATE = 128
CONV_W = 4
ATT_HEAD_DIM = 64
ATT_V_DIM = 128
ATT_BLOCK = 512
PAGES_PER_STEP = 4
DECODE_SLOTS = 3
LOG2E = 1.4426950408889634
VMEM_LIMIT = 56 * 1024 * 1024


def _cparams(sem):
    return pltpu.CompilerParams(dimension_semantics=sem, vmem_limit_bytes=VMEM_LIMIT)


def _silu(x):
    return x / (1.0 + jnp.exp2(x * -LOG2E))


def _softplus(x):
    return jnp.maximum(x, 0.0) + jnp.log(1.0 + jnp.exp(-jnp.abs(x)))


def _split_dot(v, e):
    hi = v.astype(BF16)
    lo = (v - hi.astype(F32)).astype(BF16)
    return (jnp.dot(hi, e, preferred_element_type=F32)
            + jnp.dot(lo, e, preferred_element_type=F32))


def _norm_mod(x, lnw, scale, shift):
    ms = jnp.mean(x * x, axis=-1, keepdims=True)
    y = x * lax.rsqrt(ms + EPS) * lnw
    return y * (1.0 + scale) + shift


def _adaln_kernel(c_ref, w_ref, b_ref, o_ref):
    s = _silu(c_ref[...])
    o_ref[0] = jnp.dot(s.astype(BF16), w_ref[0].astype(BF16), preferred_element_type=F32) + b_ref[0]


def _adaln(c_all, ada_w, ada_b):
    depth, d, d3 = ada_w.shape
    r = c_all.shape[0]
    return pl.pallas_call(
        _adaln_kernel,
        grid=(depth, d3 // d),
        in_specs=[pl.BlockSpec((r, d), lambda i, j: (0, 0)),
                  pl.BlockSpec((1, d, d), lambda i, j: (i, 0, j)),
                  pl.BlockSpec((1, 1, d), lambda i, j: (i, 0, j))],
        out_specs=pl.BlockSpec((1, r, d), lambda i, j: (i, 0, j)),
        out_shape=jax.ShapeDtypeStruct((depth, r, d3), F32),
        compiler_params=_cparams(("arbitrary", "arbitrary")),
        name="adaln",
    )(c_all, ada_w, ada_b.reshape(depth, 1, d3))


N_CHUNK = 512


def _inproj_ssm_kernel(x_ref, lnw_ref, sc_ref, sh_ref, wz_ref, wx_ref, wd_ref, z_ref, xbc_ref, dt_ref):
    h = _norm_mod(x_ref[...], lnw_ref[...], sc_ref[0], sh_ref[0]).astype(BF16)
    for w_ref, o_ref in ((wz_ref, z_ref), (wx_ref, xbc_ref)):
        for j in range(0, w_ref.shape[1], N_CHUNK):
            o_ref[:, j:j + N_CHUNK] = jnp.dot(h, w_ref[:, j:j + N_CHUNK], preferred_element_type=F32)
    dt_ref[...] = jnp.dot(h, wd_ref[...], preferred_element_type=F32)


def _mod_specs(mod_rows, d, tm, tiles_per_seq):
    if mod_rows == 1:
        return pl.BlockSpec((1, 1, d), lambda i: (i // tiles_per_seq, 0, 0))
    return pl.BlockSpec((1, tm, d), lambda i: (0, i, 0))


def _inproj_ssm(x, lnw, scale, shift, wz, wx, wd, tm, tiles_per_seq):
    m, d = x.shape
    full = lambda a: pl.BlockSpec(a.shape, lambda i: (0,) * a.ndim)
    row = lambda n: pl.BlockSpec((tm, n), lambda i: (i, 0))
    mod = _mod_specs(scale.shape[1], d, tm, tiles_per_seq)
    return pl.pallas_call(
        _inproj_ssm_kernel,
        grid=(m // tm,),
        in_specs=[row(d), full(lnw), mod, mod, full(wz), full(wx), full(wd)],
        out_specs=[row(wz.shape[1]), row(wx.shape[1]), row(wd.shape[1])],
        out_shape=[jax.ShapeDtypeStruct((m, wz.shape[1]), F32),
                   jax.ShapeDtypeStruct((m, wx.shape[1]), F32),
                   jax.ShapeDtypeStruct((m, wd.shape[1]), F32)],
        compiler_params=_cparams(("arbitrary",)),
        name="inproj_ssm",
    )(x, lnw, scale, shift, wz, wx, wd)


def _inproj_att_kernel(x_ref, lnw_ref, sc_ref, sh_ref, wq_ref, wkt_ref, wv_ref, wg_ref, qn_ref, kn_ref,
                       gsum_ref, q_ref, kt_ref, v_ref, g_ref, *, q_scale):
    h = _norm_mod(x_ref[...], lnw_ref[...], sc_ref[0], sh_ref[0]).astype(BF16)
    tm = h.shape[0]
    gsum = gsum_ref[...]
    width = gsum.shape[0]
    inv_hd = 1.0 / ATT_HEAD_DIM
    for j in range(0, wq_ref.shape[1], N_CHUNK):
        qc = jnp.dot(h, wq_ref[:, j:j + N_CHUNK], preferred_element_type=F32)
        for i in range(0, N_CHUNK, width):
            t = qc[:, i:i + width]
            ms = _split_dot(t * t, gsum) * inv_hd
            qn = t * lax.rsqrt(ms + EPS) * qn_ref[:, j + i:j + i + width]
            q_ref[:, j + i:j + i + width] = (qn * q_scale).astype(q_ref.dtype)
        v_ref[:, j:j + N_CHUNK] = jnp.dot(h, wv_ref[:, j:j + N_CHUNK], preferred_element_type=F32)
        g_ref[:, j:j + N_CHUNK] = jnp.dot(h, wg_ref[:, j:j + N_CHUNK], preferred_element_type=F32)
        kc = lax.dot_general(wkt_ref[j:j + N_CHUNK, :], h, (((1,), (1,)), ((), ())), preferred_element_type=F32)
        k3 = kc.reshape(N_CHUNK // ATT_HEAD_DIM, ATT_HEAD_DIM, tm)
        ms = jnp.mean(k3 * k3, axis=1, keepdims=True)
        kt_ref[0, j:j + N_CHUNK, :] = (k3 * lax.rsqrt(ms + EPS) * kn_ref[...][None]).reshape(N_CHUNK, tm)


def _inproj_att(x, lnw, scale, shift, wq, wkt, wv, wg, qn, kn_rep, gsum, tm, tiles_per_seq):
    m, d = x.shape
    n = wq.shape[1]
    seq = tm * tiles_per_seq
    full = lambda a: pl.BlockSpec(a.shape, lambda i: (0,) * a.ndim)
    row = lambda c: pl.BlockSpec((tm, c), lambda i: (i, 0))
    mod = _mod_specs(scale.shape[1], d, tm, tiles_per_seq)
    return pl.pallas_call(
        functools.partial(_inproj_att_kernel, q_scale=ATT_HEAD_DIM ** -0.5 * LOG2E),
        grid=(m // tm,),
        in_specs=[row(d), full(lnw), mod, mod, full(wq), full(wkt), full(wv), full(wg), full(qn), full(kn_rep),
                  full(gsum)],
        out_specs=[row(n), pl.BlockSpec((1, n, tm), lambda i: (i // tiles_per_seq, 0, i % tiles_per_seq)),
                   row(n), row(n)],
        out_shape=[jax.ShapeDtypeStruct((m, n), BF16), jax.ShapeDtypeStruct((m // seq, n, seq), F32),
                   jax.ShapeDtypeStruct((m, n), F32), jax.ShapeDtypeStruct((m, n), F32)],
        compiler_params=_cparams(("arbitrary",)),
        name="inproj_att",
    )(x, lnw, scale, shift, wq, wkt, wv, wg, qn, kn_rep, gsum)


def _outproj_kernel(a_ref, w_ref, x_ref, gate_ref, o_ref):
    y = jnp.dot(a_ref[...], w_ref[...], preferred_element_type=F32)
    o_ref[...] = x_ref[...] + gate_ref[0] * y


def _outproj(a, w, xres, gate, tm, tiles_per_seq):
    m, k = a.shape
    d = w.shape[1]
    return pl.pallas_call(
        _outproj_kernel,
        grid=(m // tm,),
        in_specs=[pl.BlockSpec((tm, k), lambda i: (i, 0)),
                  pl.BlockSpec((k, d), lambda i: (0, 0)),
                  pl.BlockSpec((tm, d), lambda i: (i, 0)),
                  _mod_specs(gate.shape[1], d, tm, tiles_per_seq)],
        out_specs=pl.BlockSpec((tm, d), lambda i: (i, 0)),
        out_shape=jax.ShapeDtypeStruct((m, d), F32),
        compiler_params=_cparams(("arbitrary",)),
        name="outproj",
    )(a, w, xres, gate)


def _ssd_prompt_kernel(xbc_ref, z_ref, dt_ref, cw_ref, cb_ref, dtb_ref, alog_ref, dsk_ref, nw_ref,
                       e64_ref, e128_ref, y_ref, st_ref, xp_ref, xc_ref, *, d_inner):
    q = SSD_CHUNK
    conv_dim = xbc_ref.shape[1]
    n_heads = dt_ref.shape[1]
    hpg = n_heads // SSM_GROUPS
    gw = hpg * SSM_HEAD_DIM
    c_idx = pl.program_id(1)

    @pl.when(c_idx == 0)
    def _():
        xp_ref[:, 0:8, :] = jnp.zeros((conv_dim // LANES, 8, LANES), F32)
        st_ref[...] = jnp.zeros(st_ref.shape, F32)

    for j in range(conv_dim // LANES):
        cs_ = slice(j * LANES, (j + 1) * LANES)
        xp_ref[j, 8:8 + q, :] = xbc_ref[:, cs_]
        acc = cb_ref[:, cs_] + cw_ref[CONV_W - 1:CONV_W, cs_] * xp_ref[j, 8:8 + q, :]
        for k in range(CONV_W - 1):
            acc = acc + cw_ref[k:k + 1, cs_] * xp_ref[j, 5 + k:5 + k + q, :]
        xc_ref[:, cs_] = _silu(acc)
        xp_ref[j, 0:8, :] = xp_ref[j, q:q + 8, :]

    dt = _softplus(dt_ref[...] + dtb_ref[...])
    a = -jnp.exp(alog_ref[...])
    dta = dt * a
    ti = lax.broadcasted_iota(jnp.int32, (q, q), 0)
    si = lax.broadcasted_iota(jnp.int32, (q, q), 1)
    causal = si <= ti
    tril = jnp.where(causal, 1.0, 0.0).astype(F32)
    cs = jnp.dot(tril, dta, precision=lax.Precision.HIGHEST, preferred_element_type=F32)
    cs_t = lax.dot_general(dta, tril, (((0,), (1,)), ((), ())), precision=lax.Precision.HIGHEST,
                           preferred_element_type=F32)
    cs_last = cs[q - 1:q, :]
    e64 = e64_ref[...]
    dt_e = _split_dot(dt, e64)
    ecs_e = _split_dot(jnp.exp(cs), e64)
    toend_e = _split_dot(jnp.exp(cs_last - cs), e64)
    dec_rows = _split_dot(jnp.broadcast_to(jnp.exp(cs_last), (8, n_heads)), e128_ref[...])

    lane = lax.broadcasted_iota(jnp.int32, (q, LANES), 1)
    left = lane < SSM_HEAD_DIM
    b_off = d_inner
    c_off = d_inner + SSM_GROUPS * D_STATE
    for g in range(SSM_GROUPS):
        gs = slice(g * gw, (g + 1) * gw)
        xs = xc_ref[:, gs]
        bg = xc_ref[:, b_off + g * D_STATE:b_off + (g + 1) * D_STATE].astype(BF16)
        cg = xc_ref[:, c_off + g * D_STATE:c_off + (g + 1) * D_STATE].astype(BF16)
        xdt = xs * dt_e[:, gs]
        cb = lax.dot_general(cg, bg, (((1,), (1,)), ((), ())), preferred_element_type=F32)
        y_parts = []
        for pr in range(hpg // 2):
            ms = []
            for hh in range(2):
                h = g * hpg + pr * 2 + hh
                seg = cs[:, h:h + 1] - cs_t[h:h + 1, :]
                dec = jnp.exp(jnp.where(causal, seg, NEG))
                ms.append((cb * dec).astype(BF16))
            xpair = xdt[:, pr * LANES:(pr + 1) * LANES]
            rhs = jnp.concatenate([jnp.where(left, xpair, 0.0), jnp.where(left, 0.0, xpair)], axis=0)
            y_parts.append(jnp.dot(jnp.concatenate(ms, axis=1), rhs.astype(BF16), preferred_element_type=F32))
        y = jnp.concatenate(y_parts, axis=1)
        st = st_ref[0, g * gw:(g + 1) * gw, :]
        y_off = lax.dot_general(cg, st.astype(BF16), (((1,), (1,)), ((), ())), preferred_element_type=F32)
        y = y + y_off * ecs_e[:, gs] + xs * dsk_ref[:, gs]
        upd = lax.dot_general((xdt * toend_e[:, gs]).astype(BF16), bg, (((0,), (0,)), ((), ())),
                              preferred_element_type=F32)
        for hh in range(hpg):
            h = g * hpg + hh
            rs = slice(hh * SSM_HEAD_DIM, (hh + 1) * SSM_HEAD_DIM)
            st_ref[0, g * gw + hh * SSM_HEAD_DIM:g * gw + (hh + 1) * SSM_HEAD_DIM, :] = (
                st[rs, :] * dec_rows[0:1, h * D_STATE:(h + 1) * D_STATE] + upd[rs, :])
        y = y * _silu(z_ref[:, gs])
        ms_ = jnp.mean(y * y, axis=-1, keepdims=True)
        y_ref[:, gs] = (y * lax.rsqrt(ms_ + EPS) * nw_ref[:, gs]).astype(y_ref.dtype)


def _ssd_prompt(xbc, z, dt, conv_w, conv_b, dt_bias, a_log, d_skip_e, norm_w, e64, e128, batch, seq):
    m, conv_dim = xbc.shape
    d_inner = z.shape[1]
    n_heads = dt.shape[1]
    nc = seq // SSD_CHUNK
    full = lambda a: pl.BlockSpec(a.shape, lambda b, c: (0,) * a.ndim)
    row = lambda n: pl.BlockSpec((SSD_CHUNK, n), lambda b, c: (b * nc + c, 0))
    return pl.pallas_call(
        functools.partial(_ssd_prompt_kernel, d_inner=d_inner),
        grid=(batch, nc),
        in_specs=[row(conv_dim), row(d_inner), row(n_heads), full(conv_w), full(conv_b), full(dt_bias),
                  full(a_log), full(d_skip_e), full(norm_w), full(e64), full(e128)],
        out_specs=[row(d_inner),
                   pl.BlockSpec((1, n_heads * SSM_HEAD_DIM, D_STATE), lambda b, c: (b, 0, 0))],
        out_shape=[jax.ShapeDtypeStruct((m, d_inner), BF16),
                   jax.ShapeDtypeStruct((batch, n_heads * SSM_HEAD_DIM, D_STATE), F32)],
        scratch_shapes=[pltpu.VMEM((conv_dim // LANES, SSD_CHUNK + 8, LANES), F32),
                        pltpu.VMEM((SSD_CHUNK, conv_dim), F32)],
        compiler_params=_cparams(("arbitrary", "arbitrary")),
        name="ssd_prompt",
    )(xbc, z, dt, conv_w, conv_b, dt_bias, a_log, d_skip_e, norm_w, e64, e128)


def _ssd_step_kernel(xbc_ref, z_ref, dt_ref, cst_ref, st_ref, cw_ref, cb_ref, dtb_ref, alog_ref, dsk_ref,
                     nw_ref, e64_ref, e128_ref, y_ref, cst_out_ref, st_out_ref, *, d_inner):
    n_heads = dt_ref.shape[2]
    hpg = n_heads // SSM_GROUPS
    gw = hpg * SSM_HEAD_DIM
    x_new = xbc_ref[0]
    prev = cst_ref[0]
    acc = cb_ref[...] + cw_ref[CONV_W - 1:CONV_W, :] * x_new
    for k in range(CONV_W - 1):
        acc = acc + cw_ref[k:k + 1, :] * prev[k:k + 1, :]
    xc = _silu(acc)
    cst_out_ref[0, 0:CONV_W - 2, :] = prev[1:CONV_W - 1, :]
    cst_out_ref[0, CONV_W - 2:CONV_W - 1, :] = x_new

    dt = _softplus(dt_ref[0] + dtb_ref[...])
    a = -jnp.exp(alog_ref[...])
    dec = jnp.exp(dt * a)
    dt8 = jnp.broadcast_to(dt, (8, n_heads))
    dt_e = _split_dot(dt8, e64_ref[...])[0:1, :]
    dec_rows = _split_dot(jnp.broadcast_to(dec, (8, n_heads)), e128_ref[...])
    row0 = lax.broadcasted_iota(jnp.int32, (8, D_STATE), 0) == 0
    b_off = d_inner
    c_off = d_inner + SSM_GROUPS * D_STATE
    for g in range(SSM_GROUPS):
        gs = slice(g * gw, (g + 1) * gw)
        xs = xc[:, gs]
        xdt8 = jnp.broadcast_to(xs * dt_e[:, gs], (8, gw)).astype(BF16)
        bg = xc[:, b_off + g * D_STATE:b_off + (g + 1) * D_STATE]
        cg = xc[:, c_off + g * D_STATE:c_off + (g + 1) * D_STATE]
        b8 = jnp.where(row0, jnp.broadcast_to(bg, (8, D_STATE)), 0.0).astype(BF16)
        c8 = jnp.broadcast_to(cg, (8, D_STATE)).astype(BF16)
        upd = lax.dot_general(xdt8, b8, (((0,), (0,)), ((), ())), preferred_element_type=F32)
        news = []
        for hh in range(hpg):
            h = g * hpg + hh
            rs = slice(g * gw + hh * SSM_HEAD_DIM, g * gw + (hh + 1) * SSM_HEAD_DIM)
            new = (st_ref[0, rs, :] * dec_rows[0:1, h * D_STATE:(h + 1) * D_STATE]
                   + upd[hh * SSM_HEAD_DIM:(hh + 1) * SSM_HEAD_DIM, :])
            st_out_ref[0, rs, :] = new
            news.append(new)
        st_new = jnp.concatenate(news, axis=0).astype(BF16)
        y = lax.dot_general(c8, st_new, (((1,), (1,)), ((), ())), preferred_element_type=F32)[0:1, :]
        y = y + xs * dsk_ref[:, gs]
        y = y * _silu(z_ref[0, :, gs])
        ms_ = jnp.mean(y * y, axis=-1, keepdims=True)
        y_ref[0, :, gs] = (y * lax.rsqrt(ms_ + EPS) * nw_ref[:, gs]).astype(y_ref.dtype)


def _ssd_step(xbc, z, dt, conv_state, ssm_state, conv_w, conv_b, dt_bias, a_log, d_skip_e, norm_w, e64, e128):
    bsz, conv_dim = xbc.shape
    d_inner = z.shape[1]
    n_heads = dt.shape[1]
    rows = n_heads * SSM_HEAD_DIM
    full = lambda a: pl.BlockSpec(a.shape, lambda b: (0,) * a.ndim)
    per_b = lambda *s: pl.BlockSpec((1,) + s, lambda b: (b,) + (0,) * len(s))
    return pl.pallas_call(
        functools.partial(_ssd_step_kernel, d_inner=d_inner),
        grid=(bsz,),
        in_specs=[per_b(1, conv_dim), per_b(1, d_inner), per_b(1, n_heads), per_b(CONV_W - 1, conv_dim),
                  per_b(rows, D_STATE), full(conv_w), full(conv_b), full(dt_bias), full(a_log),
                  full(d_skip_e), full(norm_w), full(e64), full(e128)],
        out_specs=[per_b(1, d_inner), per_b(CONV_W - 1, conv_dim), per_b(rows, D_STATE)],
        out_shape=[jax.ShapeDtypeStruct((bsz, 1, d_inner), BF16),
                   jax.ShapeDtypeStruct((bsz, CONV_W - 1, conv_dim), F32),
                   jax.ShapeDtypeStruct((bsz, rows, D_STATE), F32)],
        compiler_params=_cparams(("arbitrary",)),
        name="ssd_step",
    )(xbc.reshape(bsz, 1, conv_dim), z.reshape(bsz, 1, d_inner), dt.reshape(bsz, 1, n_heads), conv_state,
      ssm_state.reshape(bsz, rows, D_STATE), conv_w, conv_b, dt_bias, a_log, d_skip_e, norm_w, e64, e128)


def _attn_kernel(pt_ref, slopes_ref, lam_ref,
                 q_ref, k_ref, v_ref, g_ref, sw_ref,
                 kc_ref, vc_ref, qd_ref, knd_ref, vnd_ref, gd_ref, slope_ref, e_ref,
                 o_ref, od_ref,
                 kb_ref, vb_ref, qs_ref, m_ref, a_ref,
                 kbuf_ref, vbuf_ref, ksem_ref, vsem_ref, qb_ref, mk_ref, md_ref, ld_ref, accd_ref,
                 *, out_scale, n_pages, page_size):
    t = ATT_BLOCK
    npp = PAGES_PER_STEP
    nslot = DECODE_SLOTS
    seq = q_ref.shape[0]
    nblk = seq // t
    n_hc, width = qb_ref.shape
    n_heads = n_hc // 2
    steps_per_seq = n_pages // npp
    g_idx = pl.program_id(0) * pl.num_programs(1) + pl.program_id(1)
    n_dsteps = pl.num_programs(0) * pl.num_programs(1) * nblk
    slope = slopes_ref[pl.program_id(1)] * LOG2E
    lam = lam_ref[0]

    def page_copies(dstep):
        slot = dstep % nslot
        bs = dstep // steps_per_seq
        s = dstep % steps_per_seq
        cps = []
        for i in range(npp):
            page = pt_ref[bs, s * npp + i]
            cps.append(pltpu.make_async_copy(kc_ref.at[page], kbuf_ref.at[slot, i], ksem_ref.at[slot]))
            cps.append(pltpu.make_async_copy(vc_ref.at[page], vbuf_ref.at[slot, i], vsem_ref.at[slot]))
        return cps

    def start_pages(dstep):
        for cp in page_copies(dstep):
            cp.start()

    def wait_pages(dstep):
        for cp in page_copies(dstep):
            cp.wait()

    @pl.when(g_idx == 0)
    def _():
        for d0 in range(nslot - 1):
            start_pages(d0)
        r = lax.broadcasted_iota(jnp.int32, (n_hc, page_size * n_heads), 0)
        ln = lax.broadcasted_iota(jnp.int32, (n_hc, page_size * n_heads), 1)
        mk_ref[...] = jnp.where(ln % n_heads == r % n_heads, 1.0, 0.0).astype(BF16)

    def decode_init(dstep):
        @pl.when(dstep % steps_per_seq == 0)
        def _():
            r = lax.broadcasted_iota(jnp.int32, (n_hc, width), 0)
            ln = lax.broadcasted_iota(jnp.int32, (n_hc, width), 1)
            hc = 2 * (r % n_heads) + r // n_heads
            sel = ln // ATT_HEAD_DIM == hc
            qrow = jnp.broadcast_to(qd_ref[0].astype(F32), (n_hc, width))
            qblk = jnp.where(sel, qrow, 0.0)
            qb_ref[...] = qblk.astype(BF16)
            s_new = jnp.sum(qblk * knd_ref[0], axis=1, keepdims=True)
            md_ref[...] = jnp.broadcast_to(s_new, (n_hc, LANES))
            ld_ref[...] = jnp.ones((n_hc, LANES), F32)
            vn = vnd_ref[0]
            accd_ref[...] = jnp.concatenate([vn, vn], axis=0)

    def decode_main(dstep):
        s_idx = dstep % steps_per_seq
        slot = dstep % nslot
        qblk = qb_ref[...]
        srow = slope_ref[...]
        kofs = lax.broadcasted_iota(jnp.int32, (1, page_size), 1)
        s_parts = []
        for i in range(npp):
            kt = kbuf_ref[slot, i].reshape(width, page_size).astype(BF16)
            s = jnp.dot(qblk, kt, preferred_element_type=F32)
            kpos = (s_idx * npp + i) * page_size + kofs
            dist = (n_pages * page_size - kpos).astype(F32)
            s_parts.append(s - srow * dist)
        m_prev = md_ref[...]
        m_cur = s_parts[0].max(axis=1, keepdims=True)
        for i in range(1, npp):
            m_cur = jnp.maximum(m_cur, s_parts[i].max(axis=1, keepdims=True))
        m_new = jnp.maximum(m_prev, m_cur)
        alpha = jnp.exp2(m_prev - m_new)
        l_new = alpha * ld_ref[...]
        pv = None
        for i in range(npp):
            p = jnp.exp2(s_parts[i] - m_new)
            l_new = l_new + jnp.sum(p, axis=1, keepdims=True)
            pe = jnp.dot(p.astype(BF16), e_ref[...], preferred_element_type=F32).astype(BF16) * mk_ref[...]
            v2 = vbuf_ref[slot, i].reshape(page_size * n_heads, ATT_V_DIM).astype(BF16)
            d = jnp.dot(pe, v2, preferred_element_type=F32)
            pv = d if pv is None else pv + d
        accd_ref[...] = alpha * accd_ref[...] + pv
        ld_ref[...] = l_new
        md_ref[...] = m_new

    def decode_final(dstep):
        @pl.when(dstep % steps_per_seq == steps_per_seq - 1)
        def _():
            o = (accd_ref[0:n_heads, :] / ld_ref[0:n_heads, :]
                 - lam * (accd_ref[n_heads:n_hc, :] / ld_ref[n_heads:n_hc, :]))
            ms = jnp.mean(o * o, axis=-1, keepdims=True)
            o = o * lax.rsqrt(ms + EPS) * sw_ref[...] * out_scale
            od_ref[0] = (o * _silu(gd_ref[0])).astype(od_ref.dtype)

    for j in range(nblk):
        kb_ref[j] = k_ref[0, 0, :, :, j * t:(j + 1) * t].reshape(2 * ATT_HEAD_DIM, t).astype(BF16)
    vb_ref[:, 0:ATT_V_DIM] = v_ref[...].astype(BF16)
    vb_ref[:, ATT_V_DIM:2 * ATT_V_DIM] = jnp.ones((seq, ATT_V_DIM), BF16)
    first = lax.broadcasted_iota(jnp.int32, (t, LANES), 1) < ATT_HEAD_DIM
    kidx = lax.broadcasted_iota(jnp.int32, (1, t), 1).astype(F32)

    def q_block(qi, carry):
        dstep = g_idx * nblk + qi
        nxt = dstep + (nslot - 1)

        @pl.when(nxt < n_dsteps)
        def _():
            start_pages(nxt)

        wait_pages(dstep)
        decode_init(dstep)

        q0 = pl.multiple_of(qi * t, t)
        qp = q_ref[pl.ds(q0, t), :].astype(F32)
        qs_ref[0:t, :] = jnp.where(first, qp, 0.0).astype(BF16)
        qs_ref[t:2 * t, :] = jnp.where(first, 0.0, qp).astype(BF16)

        ri = lax.broadcasted_iota(jnp.int32, (2 * t, t), 0)
        ci = lax.broadcasted_iota(jnp.int32, (2 * t, t), 1)
        causal = ci <= jnp.where(ri >= t, ri - t, ri)
        s = jnp.dot(qs_ref[...], kb_ref[qi], preferred_element_type=F32) + slope * kidx
        s = jnp.where(causal, s, NEG)
        m = jnp.max(s, axis=1, keepdims=True)
        p = jnp.exp2(s - m)
        m_ref[...] = jnp.broadcast_to(m, (2 * t, LANES))
        a_ref[...] = jnp.dot(p.astype(BF16), vb_ref[pl.ds(q0, t), :], preferred_element_type=F32)

        decode_main(dstep)
        decode_final(dstep)

        def kv_block(j, carry2):
            k0 = pl.multiple_of(j * t, t)
            bias = slope * (kidx + (k0 - q0).astype(F32))
            s = jnp.dot(qs_ref[...], kb_ref[j], preferred_element_type=F32) + bias
            m_prev = m_ref[...]
            m_new = jnp.maximum(m_prev, jnp.max(s, axis=1, keepdims=True))
            alpha = jnp.exp2(m_prev - m_new)
            p = jnp.exp2(s - jnp.concatenate([m_new] * (t // LANES), axis=1))
            a_ref[...] = (jnp.concatenate([alpha, alpha], axis=1) * a_ref[...]
                          + jnp.dot(p.astype(BF16), vb_ref[pl.ds(k0, t), :], preferred_element_type=F32))
            m_ref[...] = m_new
            return carry2

        lax.fori_loop(0, qi, kv_block, 0)

        o = (a_ref[0:t, 0:ATT_V_DIM] / a_ref[0:t, ATT_V_DIM:]
             - lam * (a_ref[t:2 * t, 0:ATT_V_DIM] / a_ref[t:2 * t, ATT_V_DIM:]))
        ms = jnp.mean(o * o, axis=-1, keepdims=True)
        o = o * lax.rsqrt(ms + EPS) * sw_ref[...] * out_scale
        o_ref[pl.ds(q0, t), :] = (o * _silu(g_ref[pl.ds(q0, t), :])).astype(o_ref.dtype)
        return carry

    lax.fori_loop(0, nblk, q_block, 0)


def _attn_fused(page_table, slopes, lam, q, kt, v, g, subln_w, kc, vc, qd, knd, vnd, gd, slope_rows, e_tok,
                batch, seq, out_scale):
    m, width = q.shape
    n_heads = width // ATT_V_DIM
    n_hc = 2 * n_heads
    t = ATT_BLOCK
    npp = PAGES_PER_STEP
    bsz, n_pages = page_table.shape
    page_size = kc.shape[-1]
    nblk = seq // t
    assert bsz * (n_pages // npp) == batch * n_heads * nblk, "one decode page group per prompt q tile"
    gps = (n_pages // npp) // nblk
    assert gps * nblk * npp == n_pages
    blk = pl.BlockSpec((seq, ATT_V_DIM), lambda b, h, *_: (b, h))
    dseq = lambda b, h: (b * n_heads + h) // gps
    rowd = pl.BlockSpec((1, 1, width), lambda b, h, *_: (dseq(b, h), 0, 0))
    headd = pl.BlockSpec((1, n_heads, ATT_V_DIM), lambda b, h, *_: (dseq(b, h), 0, 0))
    const = lambda shape: pl.BlockSpec(shape, lambda b, h, *_: (0,) * len(shape))
    grid_spec = pltpu.PrefetchScalarGridSpec(
        num_scalar_prefetch=3,
        grid=(batch, n_heads),
        in_specs=[blk,
                  pl.BlockSpec((1, 1, 2, ATT_HEAD_DIM, seq), lambda b, h, *_: (b, h, 0, 0, 0)),
                  blk, blk, const((1, ATT_V_DIM)),
                  pl.BlockSpec(memory_space=pl.ANY), pl.BlockSpec(memory_space=pl.ANY),
                  rowd, rowd, headd, headd, const((n_hc, LANES)), const((page_size, page_size * n_heads))],
        out_specs=[blk, headd],
        scratch_shapes=[pltpu.VMEM((nblk, 2 * ATT_HEAD_DIM, t), BF16), pltpu.VMEM((seq, 2 * ATT_V_DIM), BF16),
                        pltpu.VMEM((2 * t, LANES), BF16),
                        pltpu.VMEM((2 * t, LANES), F32), pltpu.VMEM((2 * t, 2 * ATT_V_DIM), F32),
                        pltpu.VMEM((DECODE_SLOTS, npp, n_heads, 2, ATT_HEAD_DIM, page_size), F32),
                        pltpu.VMEM((DECODE_SLOTS, npp, page_size, n_heads, ATT_V_DIM), F32),
                        pltpu.SemaphoreType.DMA((DECODE_SLOTS,)), pltpu.SemaphoreType.DMA((DECODE_SLOTS,)),
                        pltpu.VMEM((n_hc, width), BF16), pltpu.VMEM((n_hc, page_size * n_heads), BF16),
                        pltpu.VMEM((n_hc, LANES), F32), pltpu.VMEM((n_hc, LANES), F32),
                        pltpu.VMEM((n_hc, ATT_V_DIM), F32)],
    )
    o, od = pl.pallas_call(
        functools.partial(_attn_kernel, out_scale=out_scale, n_pages=n_pages, page_size=page_size),
        grid_spec=grid_spec,
        out_shape=[jax.ShapeDtypeStruct((m, width), BF16), jax.ShapeDtypeStruct((bsz, n_heads, ATT_V_DIM), BF16)],
        compiler_params=_cparams(("arbitrary", "arbitrary")),
        name="attn_fused",
    )(page_table, slopes, lam, q, kt, v, g, subln_w, kc, vc,
      qd.reshape(bsz, 1, width), knd.reshape(bsz, 1, width), vnd.reshape(bsz, n_heads, ATT_V_DIM),
      gd.reshape(bsz, n_heads, ATT_V_DIM), slope_rows, e_tok)
    return o, od.reshape(bsz, width)


def _expand_matrix(n_rows, rep):
    r = lax.broadcasted_iota(jnp.int32, (n_rows, n_rows * rep), 0)
    c = lax.broadcasted_iota(jnp.int32, (n_rows, n_rows * rep), 1)
    return (c // rep == r).astype(BF16)


def kernel(x_prompt, x_sample, state_conv, state_ssm, cache_k, cache_v, page_table, c_prompt, c_sample,
           ada_w, ada_b, ln_w, ssm_w_in, ssm_conv_w, ssm_conv_b, ssm_dt_bias, ssm_a_log, ssm_d,
           ssm_norm_w, ssm_w_out, att_w_in, att_q_norm, att_k_norm, att_lambda, att_subln_w, att_w_out):
    batch, seq, d = x_prompt.shape
    dec_batch = x_sample.shape[0]
    n_ssm_heads = ssm_a_log.shape[1]
    d_inner = n_ssm_heads * SSM_HEAD_DIM
    conv_dim = ssm_conv_w.shape[2]
    att_width = att_w_out.shape[1]
    n_att_heads = att_width // ATT_V_DIM
    m = batch * seq
    tm = 256
    tm_out = 512
    tiles_per_seq = seq // tm

    mod = _adaln(jnp.concatenate([c_prompt, c_sample], axis=0), ada_w, ada_b)

    def mods(i):
        parts = [mod[i, :, k * d:(k + 1) * d] for k in range(3)]
        return ([p[:batch].reshape(batch, 1, d) for p in parts],
                [p[batch:].reshape(1, dec_batch, d) for p in parts])

    xp = x_prompt.reshape(m, d)
    xs = x_sample.reshape(dec_batch, d)

    (sh_p, sc_p, gt_p), (sh_s, sc_s, gt_s) = mods(0)
    w_in = ssm_w_in[0].astype(BF16)
    wz, wx, wd = w_in[:, :d_inner], w_in[:, d_inner:d_inner + conv_dim], w_in[:, d_inner + conv_dim:]
    lnw0 = ln_w[0].reshape(1, d)
    conv_w, conv_b = ssm_conv_w[0], ssm_conv_b[0].reshape(1, conv_dim)
    dt_bias, a_log = ssm_dt_bias[0].reshape(1, -1), ssm_a_log[0].reshape(1, -1)
    d_skip_e = jnp.repeat(ssm_d[0], SSM_HEAD_DIM).reshape(1, d_inner)
    norm_w = ssm_norm_w[0].reshape(1, d_inner)
    w_out0 = ssm_w_out[0].astype(BF16)
    e64 = _expand_matrix(n_ssm_heads, SSM_HEAD_DIM)
    e128 = _expand_matrix(n_ssm_heads, D_STATE)

    z_p, xbc_p, dt_p = _inproj_ssm(xp, lnw0, sc_p, sh_p, wz, wx, wd, tm, tiles_per_seq)
    yn_p, ssm_p = _ssd_prompt(xbc_p, z_p, dt_p, conv_w, conv_b, dt_bias, a_log, d_skip_e, norm_w, e64, e128,
                              batch, seq)
    y1_p = _outproj(yn_p, w_out0, xp, gt_p, tm_out, seq // tm_out)
    conv_p = xbc_p.reshape(batch, seq, conv_dim)[:, seq - (CONV_W - 1):, :]

    z_s, xbc_s, dt_s = _inproj_ssm(xs, lnw0, sc_s, sh_s, wz, wx, wd, dec_batch, 1)
    yn_s, conv_s, ssm_s = _ssd_step(xbc_s, z_s, dt_s, state_conv[0], state_ssm[0], conv_w, conv_b, dt_bias,
                                    a_log, d_skip_e, norm_w, e64, e128)
    y1_s = _outproj(yn_s.reshape(dec_batch, d_inner), w_out0, xs, gt_s, dec_batch, 1)

    (sh_p, sc_p, gt_p), (sh_s, sc_s, gt_s) = mods(1)
    lam_init = 0.8 - 0.6 * math.exp(-0.3 * 1)
    lmb = att_lambda[0].astype(F32)
    lam = (jnp.exp(jnp.sum(lmb[0] * lmb[1])) - jnp.exp(jnp.sum(lmb[2] * lmb[3])) + lam_init).reshape(1)
    slopes = jnp.exp2(-8.0 * jnp.arange(1, n_att_heads + 1, dtype=F32) / n_att_heads)
    wa = att_w_in[0].astype(BF16)
    wq, wk, wv, wg = (wa[:, i * att_width:(i + 1) * att_width] for i in range(4))
    wkt = wk.T
    lnw1 = ln_w[1].reshape(1, d)
    n_rep = att_width // ATT_HEAD_DIM
    qn = jnp.tile(att_q_norm[0], n_rep).reshape(1, att_width)
    gi = lax.broadcasted_iota(jnp.int32, (2 * LANES, 2 * LANES), 0) // ATT_HEAD_DIM
    gj = lax.broadcasted_iota(jnp.int32, (2 * LANES, 2 * LANES), 1) // ATT_HEAD_DIM
    gsum = (gi == gj).astype(BF16)
    subln_w = att_subln_w[0].reshape(1, ATT_V_DIM)
    w_out1 = att_w_out[0].astype(BF16)
    out_scale = 1.0 - lam_init

    def kn_rep(cols):
        return jnp.broadcast_to(att_k_norm[0][:, None], (ATT_HEAD_DIM, cols))

    q_p, kt_p, v_p, g_p = _inproj_att(y1_p, lnw1, sc_p, sh_p, wq, wkt, wv, wg, qn, kn_rep(tm), gsum, tm,
                                      tiles_per_seq)
    kt_p = kt_p.reshape(batch, n_att_heads, 2, ATT_HEAD_DIM, seq)

    q_s, kt_s, v_s, g_s = _inproj_att(y1_s, lnw1, sc_s, sh_s, wq, wkt, wv, wg, qn, kn_rep(dec_batch), gsum,
                                      dec_batch, 1)
    k_s = kt_s[0].T
    page_size = cache_k.shape[2]
    n_hc = 2 * n_att_heads
    slope_rows = jnp.broadcast_to((jnp.tile(slopes, 2) * LOG2E)[:, None], (n_hc, LANES))
    ei = lax.broadcasted_iota(jnp.int32, (page_size, page_size * n_att_heads), 0)
    ej = lax.broadcasted_iota(jnp.int32, (page_size, page_size * n_att_heads), 1) // n_att_heads
    e_tok = (ei == ej).astype(BF16)
    kc = jnp.transpose(cache_k[0], (0, 2, 3, 4, 1))
    o_p, o_s = _attn_fused(page_table, slopes, lam, q_p, kt_p, v_p, g_p, subln_w, kc, cache_v[0],
                           q_s, k_s, v_s, g_s, slope_rows, e_tok, batch, seq, out_scale)
    y2_p = _outproj(o_p, w_out1, y1_p, gt_p, tm_out, seq // tm_out)
    y2_s = _outproj(o_s, w_out1, y1_s, gt_s, dec_batch, 1)
    k_p = jnp.transpose(kt_p, (0, 4, 1, 2, 3))

    return (y2_p.reshape(batch, seq, d),
            y2_s.reshape(dec_batch, 1, d),
            conv_p[None],
            ssm_p.reshape(1, batch, n_ssm_heads, SSM_HEAD_DIM, D_STATE),
            k_p[None],
            v_p.reshape(1, batch, seq, n_att_heads, ATT_V_DIM),
            conv_s[None],
            ssm_s.reshape(1, dec_batch, n_ssm_heads, SSM_HEAD_DIM, D_STATE),
            k_s.reshape(1, dec_batch, 1, n_att_heads, 2, ATT_HEAD_DIM),
            v_s.reshape(1, dec_batch, 1, n_att_heads, ATT_V_DIM))
```

```python
import functools
import math

import jax
import jax.numpy as jnp
from jax import lax
from jax.experimental import pallas as pl
from jax.experimental.pallas import tpu as pltpu

F32 = jnp.float32
BF16 = jnp.bfloat16
EPS = 1e-6
NEG = -1e30

LANES = 128
SUBLANES = 8
SSD_CHUNK = 128
STEP_SEQS = 4
SSM_HEAD_DIM = 64
SSM_GROUPS = 8
D_STATE = 128
CONV_W = 4
ATT_HEAD_DIM = 64
ATT_V_DIM = 128
ATT_BLOCK = 512
PAGES_PER_STEP = 4
DECODE_SLOTS = 3
LOG2E = 1.4426950408889634
VMEM_LIMIT = 56 * 1024 * 1024


def _cparams(sem):
    return pltpu.CompilerParams(dimension_semantics=sem, vmem_limit_bytes=VMEM_LIMIT)


def _silu(x):
    return x / (1.0 + jnp.exp2(x * -LOG2E))


def _softplus(x):
    return jnp.maximum(x, 0.0) + jnp.log(1.0 + jnp.exp(-jnp.abs(x)))


def _split_dot(v, e):
    hi = v.astype(BF16)
    lo = (v - hi.astype(F32)).astype(BF16)
    return (jnp.dot(hi, e, preferred_element_type=F32)
            + jnp.dot(lo, e, preferred_element_type=F32))


def _split_dot_stacked(v, e2):
    hi = v.astype(BF16)
    lo = (v - hi.astype(F32)).astype(BF16)
    return jnp.dot(jnp.concatenate([hi, lo], axis=1), e2, preferred_element_type=F32)


def _norm_mod(x, lnw, scale, shift):
    ms = jnp.mean(x * x, axis=-1, keepdims=True)
    y = x * lax.rsqrt(ms + EPS) * lnw
    return y * (1.0 + scale) + shift


def _adaln_kernel(c_ref, w_ref, b_ref, o_ref):
    s = _silu(c_ref[...])
    o_ref[0] = jnp.dot(s.astype(BF16), w_ref[0].astype(BF16), preferred_element_type=F32) + b_ref[0]


def _adaln(c_all, ada_w, ada_b):
    depth, d, d3 = ada_w.shape
    r = c_all.shape[0]
    return pl.pallas_call(
        _adaln_kernel,
        grid=(depth, d3 // d),
        in_specs=[pl.BlockSpec((r, d), lambda i, j: (0, 0)),
                  pl.BlockSpec((1, d, d), lambda i, j: (i, 0, j)),
                  pl.BlockSpec((1, 1, d), lambda i, j: (i, 0, j))],
        out_specs=pl.BlockSpec((1, r, d), lambda i, j: (i, 0, j)),
        out_shape=jax.ShapeDtypeStruct((depth, r, d3), F32),
        compiler_params=_cparams(("arbitrary", "arbitrary")),
        name="adaln",
    )(c_all, ada_w, ada_b.reshape(depth, 1, d3))


N_CHUNK = 512


def _inproj_ssm_kernel(x_ref, lnw_ref, sc_ref, sh_ref, wz_ref, wx_ref, wd_ref, z_ref, xbc_ref, dt_ref):
    h = _norm_mod(x_ref[...], lnw_ref[...], sc_ref[0], sh_ref[0]).astype(BF16)
    for w_ref, o_ref in ((wz_ref, z_ref), (wx_ref, xbc_ref)):
        for j in range(0, w_ref.shape[1], N_CHUNK):
            o_ref[:, j:j + N_CHUNK] = jnp.dot(h, w_ref[:, j:j + N_CHUNK], preferred_element_type=F32)
    dt_ref[...] = jnp.dot(h, wd_ref[...], preferred_element_type=F32)


def _mod_specs(mod_rows, d, tm, tiles_per_seq):
    if mod_rows == 1:
        return pl.BlockSpec((1, 1, d), lambda i: (i // tiles_per_seq, 0, 0))
    return pl.BlockSpec((1, tm, d), lambda i: (0, i, 0))


def _inproj_ssm(x, lnw, scale, shift, wz, wx, wd, tm, tiles_per_seq):
    m, d = x.shape
    full = lambda a: pl.BlockSpec(a.shape, lambda i: (0,) * a.ndim)
    row = lambda n: pl.BlockSpec((tm, n), lambda i: (i, 0))
    mod = _mod_specs(scale.shape[1], d, tm, tiles_per_seq)
    return pl.pallas_call(
        _inproj_ssm_kernel,
        grid=(m // tm,),
        in_specs=[row(d), full(lnw), mod, mod, full(wz), full(wx), full(wd)],
        out_specs=[row(wz.shape[1]), row(wx.shape[1]), row(wd.shape[1])],
        out_shape=[jax.ShapeDtypeStruct((m, wz.shape[1]), F32),
                   jax.ShapeDtypeStruct((m, wx.shape[1]), F32),
                   jax.ShapeDtypeStruct((m, wd.shape[1]), F32)],
        compiler_params=_cparams(("arbitrary",)),
        name="inproj_ssm",
    )(x, lnw, scale, shift, wz, wx, wd)


def _inproj_att_kernel(x_ref, lnw_ref, sc_ref, sh_ref, wq_ref, wkt_ref, wv_ref, wg_ref, qn_ref, kn_ref,
                       gsum_ref, q_ref, kt_ref, v_ref, g_ref, *, q_scale):
    h = _norm_mod(x_ref[...], lnw_ref[...], sc_ref[0], sh_ref[0]).astype(BF16)
    tm = h.shape[0]
    gsum = gsum_ref[...]
    width = gsum.shape[0]
    inv_hd = 1.0 / ATT_HEAD_DIM
    for j in range(0, wq_ref.shape[1], N_CHUNK):
        qc = jnp.dot(h, wq_ref[:, j:j + N_CHUNK], preferred_element_type=F32)
        for i in range(0, N_CHUNK, width):
            t = qc[:, i:i + width]
            ms = _split_dot(t * t, gsum) * inv_hd
            qn = t * lax.rsqrt(ms + EPS) * qn_ref[:, j + i:j + i + width]
            q_ref[:, j + i:j + i + width] = (qn * q_scale).astype(q_ref.dtype)
        v_ref[:, j:j + N_CHUNK] = jnp.dot(h, wv_ref[:, j:j + N_CHUNK], preferred_element_type=F32)
        g_ref[:, j:j + N_CHUNK] = jnp.dot(h, wg_ref[:, j:j + N_CHUNK], preferred_element_type=F32)
        kc = lax.dot_general(wkt_ref[j:j + N_CHUNK, :], h, (((1,), (1,)), ((), ())), preferred_element_type=F32)
        k3 = kc.reshape(N_CHUNK // ATT_HEAD_DIM, ATT_HEAD_DIM, tm)
        ms = jnp.mean(k3 * k3, axis=1, keepdims=True)
        kt_ref[0, j:j + N_CHUNK, :] = (k3 * lax.rsqrt(ms + EPS) * kn_ref[...][None]).reshape(N_CHUNK, tm)


def _inproj_att(x, lnw, scale, shift, wq, wkt, wv, wg, qn, kn_rep, gsum, tm, tiles_per_seq):
    m, d = x.shape
    n = wq.shape[1]
    seq = tm * tiles_per_seq
    full = lambda a: pl.BlockSpec(a.shape, lambda i: (0,) * a.ndim)
    row = lambda c: pl.BlockSpec((tm, c), lambda i: (i, 0))
    mod = _mod_specs(scale.shape[1], d, tm, tiles_per_seq)
    return pl.pallas_call(
        functools.partial(_inproj_att_kernel, q_scale=ATT_HEAD_DIM ** -0.5 * LOG2E),
        grid=(m // tm,),
        in_specs=[row(d), full(lnw), mod, mod, full(wq), full(wkt), full(wv), full(wg), full(qn), full(kn_rep),
                  full(gsum)],
        out_specs=[row(n), pl.BlockSpec((1, n, tm), lambda i: (i // tiles_per_seq, 0, i % tiles_per_seq)),
                   row(n), row(n)],
        out_shape=[jax.ShapeDtypeStruct((m, n), BF16), jax.ShapeDtypeStruct((m // seq, n, seq), F32),
                   jax.ShapeDtypeStruct((m, n), F32), jax.ShapeDtypeStruct((m, n), F32)],
        compiler_params=_cparams(("arbitrary",)),
        name="inproj_att",
    )(x, lnw, scale, shift, wq, wkt, wv, wg, qn, kn_rep, gsum)


def _outproj_kernel(a_ref, w_ref, x_ref, gate_ref, o_ref):
    y = jnp.dot(a_ref[...], w_ref[...], preferred_element_type=F32)
    o_ref[...] = x_ref[...] + gate_ref[0] * y


def _outproj(a, w, xres, gate, tm, tiles_per_seq):
    m, k = a.shape
    d = w.shape[1]
    return pl.pallas_call(
        _outproj_kernel,
        grid=(m // tm,),
        in_specs=[pl.BlockSpec((tm, k), lambda i: (i, 0)),
                  pl.BlockSpec((k, d), lambda i: (0, 0)),
                  pl.BlockSpec((tm, d), lambda i: (i, 0)),
                  _mod_specs(gate.shape[1], d, tm, tiles_per_seq)],
        out_specs=pl.BlockSpec((tm, d), lambda i: (i, 0)),
        out_shape=jax.ShapeDtypeStruct((m, d), F32),
        compiler_params=_cparams(("arbitrary",)),
        name="outproj",
    )(a, w, xres, gate)


def _ssd_prompt_kernel(xbc_ref, z_ref, dt_ref, cw_ref, cb_ref, dtb_ref, alog_ref, dsk_ref, nw_ref,
                       e64_ref, e128_ref, y_ref, st_ref, xp_ref, xc_ref, *, d_inner):
    q = SSD_CHUNK
    conv_dim = xbc_ref.shape[1]
    n_heads = dt_ref.shape[1]
    hpg = n_heads // SSM_GROUPS
    gw = hpg * SSM_HEAD_DIM
    c_idx = pl.program_id(1)

    @pl.when(c_idx == 0)
    def _():
        xp_ref[:, 0:SUBLANES, :] = jnp.zeros((conv_dim // LANES, SUBLANES, LANES), F32)
        st_ref[...] = jnp.zeros(st_ref.shape, F32)

    for j in range(conv_dim // LANES):
        cs_ = slice(j * LANES, (j + 1) * LANES)
        xp_ref[j, SUBLANES:SUBLANES + q, :] = xbc_ref[:, cs_]
        acc = cb_ref[:, cs_] + cw_ref[CONV_W - 1:CONV_W, cs_] * xp_ref[j, SUBLANES:SUBLANES + q, :]
        for k in range(CONV_W - 1):
            r0 = SUBLANES - (CONV_W - 1) + k
            acc = acc + cw_ref[k:k + 1, cs_] * xp_ref[j, r0:r0 + q, :]
        xc_ref[:, cs_] = _silu(acc)
        xp_ref[j, 0:SUBLANES, :] = xp_ref[j, q:q + SUBLANES, :]

    dt = _softplus(dt_ref[...] + dtb_ref[...])
    a = -jnp.exp(alog_ref[...])
    dta = dt * a
    ti = lax.broadcasted_iota(jnp.int32, (q, q), 0)
    si = lax.broadcasted_iota(jnp.int32, (q, q), 1)
    causal = si <= ti
    tril = jnp.where(causal, 1.0, 0.0).astype(F32)
    cs = jnp.dot(tril, dta, precision=lax.Precision.HIGHEST, preferred_element_type=F32)
    cs_t = lax.dot_general(dta, tril, (((0,), (1,)), ((), ())), precision=lax.Precision.HIGHEST,
                           preferred_element_type=F32)
    cs_last = cs[q - 1:q, :]
    e64s = e64_ref[...]
    dt_e = _split_dot_stacked(dt, e64s)
    ecs_e = _split_dot_stacked(jnp.exp(cs), e64s)
    toend_e = _split_dot_stacked(jnp.exp(cs_last - cs), e64s)
    dec_rows = _split_dot(jnp.broadcast_to(jnp.exp(cs_last), (8, n_heads)), e128_ref[...])

    lane = lax.broadcasted_iota(jnp.int32, (q, LANES), 1)
    left = lane < SSM_HEAD_DIM
    b_off = d_inner
    c_off = d_inner + SSM_GROUPS * D_STATE
    for g in range(SSM_GROUPS):
        gs = slice(g * gw, (g + 1) * gw)
        xs = xc_ref[:, gs]
        bg = xc_ref[:, b_off + g * D_STATE:b_off + (g + 1) * D_STATE].astype(BF16)
        cg = xc_ref[:, c_off + g * D_STATE:c_off + (g + 1) * D_STATE].astype(BF16)
        xdt = xs * dt_e[:, gs]
        cb = lax.dot_general(cg, bg, (((1,), (1,)), ((), ())), preferred_element_type=F32)
        y_parts = []
        for pr in range(hpg // 2):
            ms = []
            for hh in range(2):
                h = g * hpg + pr * 2 + hh
                seg = cs[:, h:h + 1] - cs_t[h:h + 1, :]
                dec = jnp.exp(jnp.where(causal, seg, NEG))
                ms.append((cb * dec).astype(BF16))
            xpair = xdt[:, pr * LANES:(pr + 1) * LANES]
            rhs = jnp.concatenate([jnp.where(left, xpair, 0.0), jnp.where(left, 0.0, xpair)], axis=0)
            y_parts.append(jnp.dot(jnp.concatenate(ms, axis=1), rhs.astype(BF16), preferred_element_type=F32))
        y = jnp.concatenate(y_parts, axis=1)
        st = st_ref[0, g * gw:(g + 1) * gw, :]
        y_off = lax.dot_general(cg, st.astype(BF16), (((1,), (1,)), ((), ())), preferred_element_type=F32)
        y = y + y_off * ecs_e[:, gs] + xs * dsk_ref[:, gs]
        upd = lax.dot_general((xdt * toend_e[:, gs]).astype(BF16), bg, (((0,), (0,)), ((), ())),
                              preferred_element_type=F32)
        for hh in range(hpg):
            h = g * hpg + hh
            rs = slice(hh * SSM_HEAD_DIM, (hh + 1) * SSM_HEAD_DIM)
            st_ref[0, g * gw + hh * SSM_HEAD_DIM:g * gw + (hh + 1) * SSM_HEAD_DIM, :] = (
                st[rs, :] * dec_rows[0:1, h * D_STATE:(h + 1) * D_STATE] + upd[rs, :])
        y = y * _silu(z_ref[:, gs])
        ms_ = jnp.mean(y * y, axis=-1, keepdims=True)
        y_ref[:, gs] = (y * lax.rsqrt(ms_ + EPS) * nw_ref[:, gs]).astype(y_ref.dtype)


def _ssd_prompt(xbc, z, dt, conv_w, conv_b, dt_bias, a_log, d_skip_e, norm_w, e64, e128, batch, seq):
    m, conv_dim = xbc.shape
    d_inner = z.shape[1]
    n_heads = dt.shape[1]
    nc = seq // SSD_CHUNK
    full = lambda a: pl.BlockSpec(a.shape, lambda b, c: (0,) * a.ndim)
    row = lambda n: pl.BlockSpec((SSD_CHUNK, n), lambda b, c: (b * nc + c, 0))
    return pl.pallas_call(
        functools.partial(_ssd_prompt_kernel, d_inner=d_inner),
        grid=(batch, nc),
        in_specs=[row(conv_dim), row(d_inner), row(n_heads), full(conv_w), full(conv_b), full(dt_bias),
                  full(a_log), full(d_skip_e), full(norm_w), full(e64), full(e128)],
        out_specs=[row(d_inner),
                   pl.BlockSpec((1, n_heads * SSM_HEAD_DIM, D_STATE), lambda b, c: (b, 0, 0))],
        out_shape=[jax.ShapeDtypeStruct((m, d_inner), BF16),
                   jax.ShapeDtypeStruct((batch, n_heads * SSM_HEAD_DIM, D_STATE), F32)],
        scratch_shapes=[pltpu.VMEM((conv_dim // LANES, SSD_CHUNK + SUBLANES, LANES), F32),
                        pltpu.VMEM((SSD_CHUNK, conv_dim), F32)],
        compiler_params=_cparams(("arbitrary", "arbitrary")),
        name="ssd_prompt",
    )(xbc, z, dt, conv_w, conv_b, dt_bias, a_log, d_skip_e, norm_w, e64, e128)


def _ssd_step_kernel(xbc_ref, z_ref, dt_ref, cst_ref, st_ref, cw_ref, cb_ref, dtb_ref, alog_ref, dsk_ref,
                     nw_ref, e64_ref, e128_ref, y_ref, cst_out_ref, st_out_ref, *, d_inner):
    n_heads = dt_ref.shape[2]
    hpg = n_heads // SSM_GROUPS
    gw = hpg * SSM_HEAD_DIM
    a = -jnp.exp(alog_ref[...])
    row0 = lax.broadcasted_iota(jnp.int32, (SUBLANES, D_STATE), 0) == 0
    b_off = d_inner
    c_off = d_inner + SSM_GROUPS * D_STATE
    for r in range(xbc_ref.shape[0]):
        x_new = xbc_ref[r]
        prev = cst_ref[r]
        acc = cb_ref[...] + cw_ref[CONV_W - 1:CONV_W, :] * x_new
        for k in range(CONV_W - 1):
            acc = acc + cw_ref[k:k + 1, :] * prev[k:k + 1, :]
        xc = _silu(acc)
        cst_out_ref[r, 0:CONV_W - 2, :] = prev[1:CONV_W - 1, :]
        cst_out_ref[r, CONV_W - 2:CONV_W - 1, :] = x_new

        dt = _softplus(dt_ref[r] + dtb_ref[...])
        dec = jnp.exp(dt * a)
        dt_e = _split_dot(jnp.broadcast_to(dt, (SUBLANES, n_heads)), e64_ref[...])[0:1, :]
        dec_rows = _split_dot(jnp.broadcast_to(dec, (SUBLANES, n_heads)), e128_ref[...])
        for g in range(SSM_GROUPS):
            gs = slice(g * gw, (g + 1) * gw)
            xs = xc[:, gs]
            xdt8 = jnp.broadcast_to(xs * dt_e[:, gs], (SUBLANES, gw)).astype(BF16)
            bg = xc[:, b_off + g * D_STATE:b_off + (g + 1) * D_STATE]
            cg = xc[:, c_off + g * D_STATE:c_off + (g + 1) * D_STATE]
            b8 = jnp.where(row0, jnp.broadcast_to(bg, (SUBLANES, D_STATE)), 0.0).astype(BF16)
            c8 = jnp.broadcast_to(cg, (SUBLANES, D_STATE)).astype(BF16)
            upd = lax.dot_general(xdt8, b8, (((0,), (0,)), ((), ())), preferred_element_type=F32)
            news = []
            for hh in range(hpg):
                h = g * hpg + hh
                rs = slice(g * gw + hh * SSM_HEAD_DIM, g * gw + (hh + 1) * SSM_HEAD_DIM)
                new = (st_ref[r, rs, :] * dec_rows[0:1, h * D_STATE:(h + 1) * D_STATE]
                       + upd[hh * SSM_HEAD_DIM:(hh + 1) * SSM_HEAD_DIM, :])
                st_out_ref[r, rs, :] = new
                news.append(new)
            st_new = jnp.concatenate(news, axis=0).astype(BF16)
            y = lax.dot_general(c8, st_new, (((1,), (1,)), ((), ())), preferred_element_type=F32)[0:1, :]
            y = y + xs * dsk_ref[:, gs]
            y = y * _silu(z_ref[r, :, gs])
            ms_ = jnp.mean(y * y, axis=-1, keepdims=True)
            y_ref[r, :, gs] = (y * lax.rsqrt(ms_ + EPS) * nw_ref[:, gs]).astype(y_ref.dtype)


def _ssd_step(xbc, z, dt, conv_state, ssm_state, conv_w, conv_b, dt_bias, a_log, d_skip_e, norm_w, e64, e128):
    bsz, conv_dim = xbc.shape
    d_inner = z.shape[1]
    n_heads = dt.shape[1]
    rows = n_heads * SSM_HEAD_DIM
    full = lambda a: pl.BlockSpec(a.shape, lambda b: (0,) * a.ndim)
    per_b = lambda *s: pl.BlockSpec((STEP_SEQS,) + s, lambda b: (b,) + (0,) * len(s))
    return pl.pallas_call(
        functools.partial(_ssd_step_kernel, d_inner=d_inner),
        grid=(bsz // STEP_SEQS,),
        in_specs=[per_b(1, conv_dim), per_b(1, d_inner), per_b(1, n_heads), per_b(CONV_W - 1, conv_dim),
                  per_b(rows, D_STATE), full(conv_w), full(conv_b), full(dt_bias), full(a_log),
                  full(d_skip_e), full(norm_w), full(e64), full(e128)],
        out_specs=[per_b(1, d_inner), per_b(CONV_W - 1, conv_dim), per_b(rows, D_STATE)],
        out_shape=[jax.ShapeDtypeStruct((bsz, 1, d_inner), BF16),
                   jax.ShapeDtypeStruct((bsz, CONV_W - 1, conv_dim), F32),
                   jax.ShapeDtypeStruct((bsz, rows, D_STATE), F32)],
        compiler_params=_cparams(("arbitrary",)),
        name="ssd_step",
    )(xbc.reshape(bsz, 1, conv_dim), z.reshape(bsz, 1, d_inner), dt.reshape(bsz, 1, n_heads), conv_state,
      ssm_state.reshape(bsz, rows, D_STATE), conv_w, conv_b, dt_bias, a_log, d_skip_e, norm_w, e64, e128)


def _attn_kernel(pt_ref, slopes_ref, lam_ref,
                 q_ref, k_ref, v_ref, g_ref, sw_ref,
                 kc_ref, vc_ref, qd_ref, knd_ref, vnd_ref, gd_ref, slope_ref, e_ref,
                 o_ref, od_ref,
                 kb_ref, vb_ref, qs_ref, m_ref, a_ref,
                 kbuf_ref, vbuf_ref, ksem_ref, vsem_ref, qb_ref, mk_ref, md_ref, ld_ref, accd_ref,
                 *, out_scale, n_pages, page_size):
    t = ATT_BLOCK
    npp = PAGES_PER_STEP
    nslot = DECODE_SLOTS
    seq = q_ref.shape[0]
    nblk = seq // t
    n_hc, width = qb_ref.shape
    n_heads = n_hc // 2
    steps_per_seq = n_pages // npp
    g_idx = pl.program_id(0) * pl.num_programs(1) + pl.program_id(1)
    n_dsteps = pl.num_programs(0) * pl.num_programs(1) * nblk
    slope = slopes_ref[pl.program_id(1)] * LOG2E
    lam = lam_ref[0]

    def page_copies(dstep):
        slot = dstep % nslot
        bs = dstep // steps_per_seq
        s = dstep % steps_per_seq
        cps = []
        for i in range(npp):
            page = pt_ref[bs, s * npp + i]
            cps.append(pltpu.make_async_copy(kc_ref.at[page], kbuf_ref.at[slot, i], ksem_ref.at[slot]))
            cps.append(pltpu.make_async_copy(vc_ref.at[page], vbuf_ref.at[slot, i], vsem_ref.at[slot]))
        return cps

    def start_pages(dstep):
        for cp in page_copies(dstep):
            cp.start()

    def wait_pages(dstep):
        for cp in page_copies(dstep):
            cp.wait()

    @pl.when(g_idx == 0)
    def _():
        for d0 in range(nslot - 1):
            start_pages(d0)
        r = lax.broadcasted_iota(jnp.int32, (n_hc, page_size * n_heads), 0)
        ln = lax.broadcasted_iota(jnp.int32, (n_hc, page_size * n_heads), 1)
        mk_ref[...] = jnp.where(ln % n_heads == r % n_heads, 1.0, 0.0).astype(BF16)
        vb_ref[:, ATT_V_DIM:2 * ATT_V_DIM] = jnp.ones((seq, ATT_V_DIM), BF16)

    def decode_init(dstep):
        @pl.when(dstep % steps_per_seq == 0)
        def _():
            r = lax.broadcasted_iota(jnp.int32, (n_hc, width), 0)
            ln = lax.broadcasted_iota(jnp.int32, (n_hc, width), 1)
            hc = 2 * (r % n_heads) + r // n_heads
            sel = ln // ATT_HEAD_DIM == hc
            qrow = jnp.broadcast_to(qd_ref[0].astype(F32), (n_hc, width))
            qblk = jnp.where(sel, qrow, 0.0)
            qb_ref[...] = qblk.astype(BF16)
            s_new = jnp.sum(qblk * knd_ref[0], axis=1, keepdims=True)
            md_ref[...] = jnp.broadcast_to(s_new, (n_hc, LANES))
            ld_ref[...] = jnp.ones((n_hc, LANES), F32)
            vn = vnd_ref[0]
            accd_ref[...] = jnp.concatenate([vn, vn], axis=0)

    def decode_main(dstep):
        s_idx = dstep % steps_per_seq
        slot = dstep % nslot
        qblk = qb_ref[...]
        srow = slope_ref[...]
        kofs = lax.broadcasted_iota(jnp.int32, (1, page_size), 1)
        s_parts = []
        for i in range(npp):
            kt = kbuf_ref[slot, i].reshape(width, page_size).astype(BF16)
            s = jnp.dot(qblk, kt, preferred_element_type=F32)
            kpos = (s_idx * npp + i) * page_size + kofs
            dist = (n_pages * page_size - kpos).astype(F32)
            s_parts.append(s - srow * dist)
        m_prev = md_ref[...]
        m_cur = s_parts[0].max(axis=1, keepdims=True)
        for i in range(1, npp):
            m_cur = jnp.maximum(m_cur, s_parts[i].max(axis=1, keepdims=True))
        m_new = jnp.maximum(m_prev, m_cur)
        alpha = jnp.exp2(m_prev - m_new)
        l_new = alpha * ld_ref[...]
        pv = None
        for i in range(npp):
            p = jnp.exp2(s_parts[i] - m_new)
            l_new = l_new + jnp.sum(p, axis=1, keepdims=True)
            pe = jnp.dot(p.astype(BF16), e_ref[...], preferred_element_type=F32).astype(BF16) * mk_ref[...]
            v2 = vbuf_ref[slot, i].reshape(page_size * n_heads, ATT_V_DIM).astype(BF16)
            d = jnp.dot(pe, v2, preferred_element_type=F32)
            pv = d if pv is None else pv + d
        accd_ref[...] = alpha * accd_ref[...] + pv
        ld_ref[...] = l_new
        md_ref[...] = m_new

    def decode_final(dstep):
        @pl.when(dstep % steps_per_seq == steps_per_seq - 1)
        def _():
            o = (accd_ref[0:n_heads, :] / ld_ref[0:n_heads, :]
                 - lam * (accd_ref[n_heads:n_hc, :] / ld_ref[n_heads:n_hc, :]))
            ms = jnp.mean(o * o, axis=-1, keepdims=True)
            o = o * lax.rsqrt(ms + EPS) * sw_ref[...] * out_scale
            od_ref[0] = (o * _silu(gd_ref[0])).astype(od_ref.dtype)

    for j in range(nblk):
        kb_ref[j] = k_ref[0, 0, :, :, j * t:(j + 1) * t].reshape(2 * ATT_HEAD_DIM, t).astype(BF16)
    vb_ref[:, 0:ATT_V_DIM] = v_ref[...].astype(BF16)
    first = lax.broadcasted_iota(jnp.int32, (t, LANES), 1) < ATT_HEAD_DIM
    kidx = lax.broadcasted_iota(jnp.int32, (1, t), 1).astype(F32)

    def q_block(qi, carry):
        dstep = g_idx * nblk + qi
        nxt = dstep + (nslot - 1)

        @pl.when(nxt < n_dsteps)
        def _():
            start_pages(nxt)

        wait_pages(dstep)
        decode_init(dstep)

        q0 = pl.multiple_of(qi * t, t)
        qp = q_ref[pl.ds(q0, t), :].astype(F32)
        qs_ref[0:t, :] = jnp.where(first, qp, 0.0).astype(BF16)
        qs_ref[t:2 * t, :] = jnp.where(first, 0.0, qp).astype(BF16)

        ri = lax.broadcasted_iota(jnp.int32, (2 * t, t), 0)
        ci = lax.broadcasted_iota(jnp.int32, (2 * t, t), 1)
        causal = ci <= jnp.where(ri >= t, ri - t, ri)
        s = jnp.dot(qs_ref[...], kb_ref[qi], preferred_element_type=F32) + slope * kidx
        s = jnp.where(causal, s, NEG)
        m = jnp.max(s, axis=1, keepdims=True)
        p = jnp.exp2(s - m)
        m_ref[...] = jnp.broadcast_to(m, (2 * t, LANES))
        a_ref[...] = jnp.dot(p.astype(BF16), vb_ref[pl.ds(q0, t), :], preferred_element_type=F32)

        decode_main(dstep)
        decode_final(dstep)

        def kv_block(j, carry2):
            k0 = pl.multiple_of(j * t, t)
            bias = slope * (kidx + (k0 - q0).astype(F32))
            s = jnp.dot(qs_ref[...], kb_ref[j], preferred_element_type=F32) + bias
            m_prev = m_ref[...]
            m_new = jnp.maximum(m_prev, jnp.max(s, axis=1, keepdims=True))
            alpha = jnp.exp2(m_prev - m_new)
            p = jnp.exp2(s - jnp.concatenate([m_new] * (t // LANES), axis=1))
            a_ref[...] = (jnp.concatenate([alpha, alpha], axis=1) * a_ref[...]
                          + jnp.dot(p.astype(BF16), vb_ref[pl.ds(k0, t), :], preferred_element_type=F32))
            m_ref[...] = m_new
            return carry2

        lax.fori_loop(0, qi, kv_block, 0)

        o = (a_ref[0:t, 0:ATT_V_DIM] / a_ref[0:t, ATT_V_DIM:]
             - lam * (a_ref[t:2 * t, 0:ATT_V_DIM] / a_ref[t:2 * t, ATT_V_DIM:]))
        ms = jnp.mean(o * o, axis=-1, keepdims=True)
        o = o * lax.rsqrt(ms + EPS) * sw_ref[...] * out_scale
        o_ref[pl.ds(q0, t), :] = (o * _silu(g_ref[pl.ds(q0, t), :])).astype(o_ref.dtype)
        return carry

    lax.fori_loop(0, nblk, q_block, 0)


def _attn_fused(page_table, slopes, lam, q, kt, v, g, subln_w, kc, vc, qd, knd, vnd, gd, slope_rows, e_tok,
                batch, seq, out_scale):
    m, width = q.shape
    n_heads = width // ATT_V_DIM
    n_hc = 2 * n_heads
    t = ATT_BLOCK
    npp = PAGES_PER_STEP
    bsz, n_pages = page_table.shape
    page_size = kc.shape[-1]
    nblk = seq // t
    assert bsz * (n_pages // npp) == batch * n_heads * nblk, "one decode page group per prompt q tile"
    gps = (n_pages // npp) // nblk
    assert gps * nblk * npp == n_pages
    blk = pl.BlockSpec((seq, ATT_V_DIM), lambda b, h, *_: (b, h))
    dseq = lambda b, h: (b * n_heads + h) // gps
    rowd = pl.BlockSpec((1, 1, width), lambda b, h, *_: (dseq(b, h), 0, 0))
    headd = pl.BlockSpec((1, n_heads, ATT_V_DIM), lambda b, h, *_: (dseq(b, h), 0, 0))
    const = lambda shape: pl.BlockSpec(shape, lambda b, h, *_: (0,) * len(shape))
    grid_spec = pltpu.PrefetchScalarGridSpec(
        num_scalar_prefetch=3,
        grid=(batch, n_heads),
        in_specs=[blk,
                  pl.BlockSpec((1, 1, 2, ATT_HEAD_DIM, seq), lambda b, h, *_: (b, h, 0, 0, 0)),
                  blk, blk, const((1, ATT_V_DIM)),
                  pl.BlockSpec(memory_space=pl.ANY), pl.BlockSpec(memory_space=pl.ANY),
                  rowd, rowd, headd, headd, const((n_hc, LANES)), const((page_size, page_size * n_heads))],
        out_specs=[blk, headd],
        scratch_shapes=[pltpu.VMEM((nblk, 2 * ATT_HEAD_DIM, t), BF16), pltpu.VMEM((seq, 2 * ATT_V_DIM), BF16),
                        pltpu.VMEM((2 * t, LANES), BF16),
                        pltpu.VMEM((2 * t, LANES), F32), pltpu.VMEM((2 * t, 2 * ATT_V_DIM), F32),
                        pltpu.VMEM((DECODE_SLOTS, npp, n_heads, 2, ATT_HEAD_DIM, page_size), F32),
                        pltpu.VMEM((DECODE_SLOTS, npp, page_size, n_heads, ATT_V_DIM), F32),
                        pltpu.SemaphoreType.DMA((DECODE_SLOTS,)), pltpu.SemaphoreType.DMA((DECODE_SLOTS,)),
                        pltpu.VMEM((n_hc, width), BF16), pltpu.VMEM((n_hc, page_size * n_heads), BF16),
                        pltpu.VMEM((n_hc, LANES), F32), pltpu.VMEM((n_hc, LANES), F32),
                        pltpu.VMEM((n_hc, ATT_V_DIM), F32)],
    )
    o, od = pl.pallas_call(
        functools.partial(_attn_kernel, out_scale=out_scale, n_pages=n_pages, page_size=page_size),
        grid_spec=grid_spec,
        out_shape=[jax.ShapeDtypeStruct((m, width), BF16), jax.ShapeDtypeStruct((bsz, n_heads, ATT_V_DIM), BF16)],
        compiler_params=_cparams(("arbitrary", "arbitrary")),
        name="attn_fused",
    )(page_table, slopes, lam, q, kt, v, g, subln_w, kc, vc,
      qd.reshape(bsz, 1, width), knd.reshape(bsz, 1, width), vnd.reshape(bsz, n_heads, ATT_V_DIM),
      gd.reshape(bsz, n_heads, ATT_V_DIM), slope_rows, e_tok)
    return o, od.reshape(bsz, width)


def _expand_matrix(n_rows, rep):
    r = lax.broadcasted_iota(jnp.int32, (n_rows, n_rows * rep), 0)
    c = lax.broadcasted_iota(jnp.int32, (n_rows, n_rows * rep), 1)
    return (c // rep == r).astype(BF16)


def kernel(x_prompt, x_sample, state_conv, state_ssm, cache_k, cache_v, page_table, c_prompt, c_sample,
           ada_w, ada_b, ln_w, ssm_w_in, ssm_conv_w, ssm_conv_b, ssm_dt_bias, ssm_a_log, ssm_d,
           ssm_norm_w, ssm_w_out, att_w_in, att_q_norm, att_k_norm, att_lambda, att_subln_w, att_w_out):
    batch, seq, d = x_prompt.shape
    dec_batch = x_sample.shape[0]
    n_ssm_heads = ssm_a_log.shape[1]
    d_inner = n_ssm_heads * SSM_HEAD_DIM
    conv_dim = ssm_conv_w.shape[2]
    att_width = att_w_out.shape[1]
    n_att_heads = att_width // ATT_V_DIM
    m = batch * seq
    tm = 256
    tm_out = 1024
    tiles_per_seq = seq // tm

    mod = _adaln(jnp.concatenate([c_prompt, c_sample], axis=0), ada_w, ada_b)

    def mods(i):
        parts = [mod[i, :, k * d:(k + 1) * d] for k in range(3)]
        return ([p[:batch].reshape(batch, 1, d) for p in parts],
                [p[batch:].reshape(1, dec_batch, d) for p in parts])

    xp = x_prompt.reshape(m, d)
    xs = x_sample.reshape(dec_batch, d)

    (sh_p, sc_p, gt_p), (sh_s, sc_s, gt_s) = mods(0)
    w_in = ssm_w_in[0].astype(BF16)
    wz, wx, wd = w_in[:, :d_inner], w_in[:, d_inner:d_inner + conv_dim], w_in[:, d_inner + conv_dim:]
    lnw0 = ln_w[0].reshape(1, d)
    conv_w, conv_b = ssm_conv_w[0], ssm_conv_b[0].reshape(1, conv_dim)
    dt_bias, a_log = ssm_dt_bias[0].reshape(1, -1), ssm_a_log[0].reshape(1, -1)
    d_skip_e = jnp.repeat(ssm_d[0], SSM_HEAD_DIM).reshape(1, d_inner)
    norm_w = ssm_norm_w[0].reshape(1, d_inner)
    w_out0 = ssm_w_out[0].astype(BF16)
    e64 = _expand_matrix(n_ssm_heads, SSM_HEAD_DIM)
    e128 = _expand_matrix(n_ssm_heads, D_STATE)

    z_p, xbc_p, dt_p = _inproj_ssm(xp, lnw0, sc_p, sh_p, wz, wx, wd, tm, tiles_per_seq)
    yn_p, ssm_p = _ssd_prompt(xbc_p, z_p, dt_p, conv_w, conv_b, dt_bias, a_log, d_skip_e, norm_w,
                              jnp.concatenate([e64, e64], axis=0), e128, batch, seq)
    y1_p = _outproj(yn_p, w_out0, xp, gt_p, tm_out, seq // tm_out)
    conv_p = xbc_p.reshape(batch, seq, conv_dim)[:, seq - (CONV_W - 1):, :]

    z_s, xbc_s, dt_s = _inproj_ssm(xs, lnw0, sc_s, sh_s, wz, wx, wd, dec_batch, 1)
    yn_s, conv_s, ssm_s = _ssd_step(xbc_s, z_s, dt_s, state_conv[0], state_ssm[0], conv_w, conv_b, dt_bias,
                                    a_log, d_skip_e, norm_w, e64, e128)
    y1_s = _outproj(yn_s.reshape(dec_batch, d_inner), w_out0, xs, gt_s, dec_batch, 1)

    (sh_p, sc_p, gt_p), (sh_s, sc_s, gt_s) = mods(1)
    lam_init = 0.8 - 0.6 * math.exp(-0.3 * 1)
    lmb = att_lambda[0].astype(F32)
    lam = (jnp.exp(jnp.sum(lmb[0] * lmb[1])) - jnp.exp(jnp.sum(lmb[2] * lmb[3])) + lam_init).reshape(1)
    slopes = jnp.exp2(-8.0 * jnp.arange(1, n_att_heads + 1, dtype=F32) / n_att_heads)
    wa = att_w_in[0].astype(BF16)
    wq, wk, wv, wg = (wa[:, i * att_width:(i + 1) * att_width] for i in range(4))
    wkt = wk.T
    lnw1 = ln_w[1].reshape(1, d)
    n_rep = att_width // ATT_HEAD_DIM
    qn = jnp.tile(att_q_norm[0], n_rep).reshape(1, att_width)
    gi = lax.broadcasted_iota(jnp.int32, (2 * LANES, 2 * LANES), 0) // ATT_HEAD_DIM
    gj = lax.broadcasted_iota(jnp.int32, (2 * LANES, 2 * LANES), 1) // ATT_HEAD_DIM
    gsum = (gi == gj).astype(BF16)
    subln_w = att_subln_w[0].reshape(1, ATT_V_DIM)
    w_out1 = att_w_out[0].astype(BF16)
    out_scale = 1.0 - lam_init

    def kn_rep(cols):
        return jnp.broadcast_to(att_k_norm[0][:, None], (ATT_HEAD_DIM, cols))

    q_p, kt_p, v_p, g_p = _inproj_att(y1_p, lnw1, sc_p, sh_p, wq, wkt, wv, wg, qn, kn_rep(tm), gsum, tm,
                                      tiles_per_seq)
    kt_p = kt_p.reshape(batch, n_att_heads, 2, ATT_HEAD_DIM, seq)

    q_s, kt_s, v_s, g_s = _inproj_att(y1_s, lnw1, sc_s, sh_s, wq, wkt, wv, wg, qn, kn_rep(dec_batch), gsum,
                                      dec_batch, 1)
    k_s = kt_s[0].T
    page_size = cache_k.shape[2]
    n_hc = 2 * n_att_heads
    slope_rows = jnp.broadcast_to((jnp.tile(slopes, 2) * LOG2E)[:, None], (n_hc, LANES))
    ei = lax.broadcasted_iota(jnp.int32, (page_size, page_size * n_att_heads), 0)
    ej = lax.broadcasted_iota(jnp.int32, (page_size, page_size * n_att_heads), 1) // n_att_heads
    e_tok = (ei == ej).astype(BF16)
    kc = jnp.transpose(cache_k[0], (0, 2, 3, 4, 1))
    o_p, o_s = _attn_fused(page_table, slopes, lam, q_p, kt_p, v_p, g_p, subln_w, kc, cache_v[0],
                           q_s, k_s, v_s, g_s, slope_rows, e_tok, batch, seq, out_scale)
    y2_p = _outproj(o_p, w_out1, y1_p, gt_p, tm_out, seq // tm_out)
    y2_s = _outproj(o_s, w_out1, y1_s, gt_s, dec_batch, 1)
    k_p = jnp.transpose(kt_p, (0, 4, 1, 2, 3))

    return (y2_p.reshape(batch, seq, d),
            y2_s.reshape(dec_batch, 1, d),
            conv_p[None],
            ssm_p.reshape(1, batch, n_ssm_heads, SSM_HEAD_DIM, D_STATE),
            k_p[None],
            v_p.reshape(1, batch, seq, n_att_heads, ATT_V_DIM),
            conv_s[None],
            ssm_s.reshape(1, dec_batch, n_ssm_heads, SSM_HEAD_DIM, D_STATE),
            k_s.reshape(1, dec_batch, 1, n_att_heads, 2, ATT_HEAD_DIM),
            v_s.reshape(1, dec_batch, 1, n_att_heads, ATT_V_DIM))
```

```python
import functools
import math

import jax
import jax.numpy as jnp
from jax import lax
from jax.experimental import pallas as pl
from jax.experimental.pallas import tpu as pltpu

F32 = jnp.float32
BF16 = jnp.bfloat16
EPS = 1e-6
NEG = -1e30

LANES = 128
SUBLANES = 8
SSD_CHUNK = 128
STEP_SEQS = 4
SSM_HEAD_DIM = 64
SSM_GROUPS = 8
D_STATE = 128
CONV_W = 4
ATT_HEAD_DIM = 64
ATT_V_DIM = 128
ATT_BLOCK = 512
PAGES_PER_STEP = 4
DECODE_SLOTS = 3
LOG2E = 1.4426950408889634
VMEM_LIMIT = 56 * 1024 * 1024


def _cparams(sem):
    return pltpu.CompilerParams(dimension_semantics=sem, vmem_limit_bytes=VMEM_LIMIT)


def _silu(x):
    return x / (1.0 + jnp.exp2(x * -LOG2E))


def _softplus(x):
    return jnp.maximum(x, 0.0) + jnp.log(1.0 + jnp.exp(-jnp.abs(x)))


def _split_dot(v, e):
    hi = v.astype(BF16)
    lo = (v - hi.astype(F32)).astype(BF16)
    return (jnp.dot(hi, e, preferred_element_type=F32)
            + jnp.dot(lo, e, preferred_element_type=F32))


def _split_dot_stacked(v, e2):
    hi = v.astype(BF16)
    lo = (v - hi.astype(F32)).astype(BF16)
    return jnp.dot(jnp.concatenate([hi, lo], axis=1), e2, preferred_element_type=F32)


def _norm_mod(x, lnw, scale, shift):
    ms = jnp.mean(x * x, axis=-1, keepdims=True)
    y = x * lax.rsqrt(ms + EPS) * lnw
    return y * (1.0 + scale) + shift


def _adaln_kernel(c_ref, w_ref, b_ref, o_ref):
    s = _silu(c_ref[...])
    o_ref[0] = jnp.dot(s.astype(BF16), w_ref[0].astype(BF16), preferred_element_type=F32) + b_ref[0]


def _adaln(c_all, ada_w, ada_b):
    depth, d, d3 = ada_w.shape
    r = c_all.shape[0]
    return pl.pallas_call(
        _adaln_kernel,
        grid=(depth, d3 // d),
        in_specs=[pl.BlockSpec((r, d), lambda i, j: (0, 0)),
                  pl.BlockSpec((1, d, d), lambda i, j: (i, 0, j)),
                  pl.BlockSpec((1, 1, d), lambda i, j: (i, 0, j))],
        out_specs=pl.BlockSpec((1, r, d), lambda i, j: (i, 0, j)),
        out_shape=jax.ShapeDtypeStruct((depth, r, d3), F32),
        compiler_params=_cparams(("arbitrary", "arbitrary")),
        name="adaln",
    )(c_all, ada_w, ada_b.reshape(depth, 1, d3))


N_CHUNK = 512


def _inproj_ssm_kernel(x_ref, lnw_ref, sc_ref, sh_ref, wz_ref, wx_ref, wd_ref, z_ref, xbc_ref, dt_ref):
    h = _norm_mod(x_ref[...], lnw_ref[...], sc_ref[0], sh_ref[0]).astype(BF16)
    for w_ref, o_ref in ((wz_ref, z_ref), (wx_ref, xbc_ref)):
        for j in range(0, w_ref.shape[1], N_CHUNK):
            o_ref[:, j:j + N_CHUNK] = jnp.dot(h, w_ref[:, j:j + N_CHUNK], preferred_element_type=F32)
    dt_ref[...] = jnp.dot(h, wd_ref[...], preferred_element_type=F32)


def _mod_specs(mod_rows, d, tm, tiles_per_seq):
    if mod_rows == 1:
        return pl.BlockSpec((1, 1, d), lambda i: (i // tiles_per_seq, 0, 0))
    return pl.BlockSpec((1, tm, d), lambda i: (0, i, 0))


def _inproj_ssm(x, lnw, scale, shift, wz, wx, wd, tm, tiles_per_seq):
    m, d = x.shape
    full = lambda a: pl.BlockSpec(a.shape, lambda i: (0,) * a.ndim, pipeline_mode=pl.Buffered(1))
    row = lambda n: pl.BlockSpec((tm, n), lambda i: (i, 0))
    mod = _mod_specs(scale.shape[1], d, tm, tiles_per_seq)
    return pl.pallas_call(
        _inproj_ssm_kernel,
        grid=(m // tm,),
        in_specs=[row(d), full(lnw), mod, mod, full(wz), full(wx), full(wd)],
        out_specs=[row(wz.shape[1]), row(wx.shape[1]), row(wd.shape[1])],
        out_shape=[jax.ShapeDtypeStruct((m, wz.shape[1]), F32),
                   jax.ShapeDtypeStruct((m, wx.shape[1]), F32),
                   jax.ShapeDtypeStruct((m, wd.shape[1]), F32)],
        compiler_params=_cparams(("arbitrary",)),
        name="inproj_ssm",
    )(x, lnw, scale, shift, wz, wx, wd)


def _inproj_att_kernel(x_ref, lnw_ref, sc_ref, sh_ref, wq_ref, wkt_ref, wv_ref, wg_ref, qn_ref, kn_ref,
                       gsum_ref, q_ref, kt_ref, v_ref, g_ref, *, q_scale):
    h = _norm_mod(x_ref[...], lnw_ref[...], sc_ref[0], sh_ref[0]).astype(BF16)
    tm = h.shape[0]
    gsum = gsum_ref[...]
    width = gsum.shape[0]
    inv_hd = 1.0 / ATT_HEAD_DIM
    for j in range(0, wq_ref.shape[1], N_CHUNK):
        qc = jnp.dot(h, wq_ref[:, j:j + N_CHUNK], preferred_element_type=F32)
        for i in range(0, N_CHUNK, width):
            t = qc[:, i:i + width]
            ms = _split_dot(t * t, gsum) * inv_hd
            qn = t * lax.rsqrt(ms + EPS) * qn_ref[:, j + i:j + i + width]
            q_ref[:, j + i:j + i + width] = (qn * q_scale).astype(q_ref.dtype)
        v_ref[:, j:j + N_CHUNK] = jnp.dot(h, wv_ref[:, j:j + N_CHUNK], preferred_element_type=F32)
        g_ref[:, j:j + N_CHUNK] = jnp.dot(h, wg_ref[:, j:j + N_CHUNK], preferred_element_type=F32)
        kc = lax.dot_general(wkt_ref[j:j + N_CHUNK, :], h, (((1,), (1,)), ((), ())), preferred_element_type=F32)
        k3 = kc.reshape(N_CHUNK // ATT_HEAD_DIM, ATT_HEAD_DIM, tm)
        ms = jnp.mean(k3 * k3, axis=1, keepdims=True)
        kt_ref[0, j:j + N_CHUNK, :] = (k3 * lax.rsqrt(ms + EPS) * kn_ref[...][None]).reshape(N_CHUNK, tm)


def _inproj_att(x, lnw, scale, shift, wq, wkt, wv, wg, qn, kn_rep, gsum, tm, tiles_per_seq):
    m, d = x.shape
    n = wq.shape[1]
    seq = tm * tiles_per_seq
    full = lambda a: pl.BlockSpec(a.shape, lambda i: (0,) * a.ndim, pipeline_mode=pl.Buffered(1))
    row = lambda c: pl.BlockSpec((tm, c), lambda i: (i, 0))
    mod = _mod_specs(scale.shape[1], d, tm, tiles_per_seq)
    return pl.pallas_call(
        functools.partial(_inproj_att_kernel, q_scale=ATT_HEAD_DIM ** -0.5 * LOG2E),
        grid=(m // tm,),
        in_specs=[row(d), full(lnw), mod, mod, full(wq), full(wkt), full(wv), full(wg), full(qn), full(kn_rep),
                  full(gsum)],
        out_specs=[row(n), pl.BlockSpec((1, n, tm), lambda i: (i // tiles_per_seq, 0, i % tiles_per_seq)),
                   row(n), row(n)],
        out_shape=[jax.ShapeDtypeStruct((m, n), BF16), jax.ShapeDtypeStruct((m // seq, n, seq), F32),
                   jax.ShapeDtypeStruct((m, n), F32), jax.ShapeDtypeStruct((m, n), F32)],
        compiler_params=_cparams(("arbitrary",)),
        name="inproj_att",
    )(x, lnw, scale, shift, wq, wkt, wv, wg, qn, kn_rep, gsum)


def _outproj_kernel(a_ref, w_ref, x_ref, gate_ref, o_ref):
    y = jnp.dot(a_ref[...], w_ref[...], preferred_element_type=F32)
    o_ref[...] = x_ref[...] + gate_ref[0] * y


def _outproj(a, w, xres, gate, tm, tiles_per_seq):
    m, k = a.shape
    d = w.shape[1]
    return pl.pallas_call(
        _outproj_kernel,
        grid=(m // tm,),
        in_specs=[pl.BlockSpec((tm, k), lambda i: (i, 0)),
                  pl.BlockSpec((k, d), lambda i: (0, 0)),
                  pl.BlockSpec((tm, d), lambda i: (i, 0)),
                  _mod_specs(gate.shape[1], d, tm, tiles_per_seq)],
        out_specs=pl.BlockSpec((tm, d), lambda i: (i, 0)),
        out_shape=jax.ShapeDtypeStruct((m, d), F32),
        compiler_params=_cparams(("arbitrary",)),
        name="outproj",
    )(a, w, xres, gate)


def _ssd_prompt_kernel(xbc_ref, z_ref, dt_ref, cw_ref, cb_ref, dtb_ref, alog_ref, dsk_ref, nw_ref,
                       e64_ref, e128_ref, y_ref, st_ref, xp_ref, xc_ref, *, d_inner):
    q = SSD_CHUNK
    conv_dim = xbc_ref.shape[1]
    n_heads = dt_ref.shape[1]
    hpg = n_heads // SSM_GROUPS
    gw = hpg * SSM_HEAD_DIM
    c_idx = pl.program_id(1)

    @pl.when(c_idx == 0)
    def _():
        xp_ref[:, 0:SUBLANES, :] = jnp.zeros((conv_dim // LANES, SUBLANES, LANES), F32)
        st_ref[...] = jnp.zeros(st_ref.shape, F32)

    for j in range(conv_dim // LANES):
        cs_ = slice(j * LANES, (j + 1) * LANES)
        xp_ref[j, SUBLANES:SUBLANES + q, :] = xbc_ref[:, cs_]
        acc = cb_ref[:, cs_] + cw_ref[CONV_W - 1:CONV_W, cs_] * xp_ref[j, SUBLANES:SUBLANES + q, :]
        for k in range(CONV_W - 1):
            r0 = SUBLANES - (CONV_W - 1) + k
            acc = acc + cw_ref[k:k + 1, cs_] * xp_ref[j, r0:r0 + q, :]
        xc_ref[:, cs_] = _silu(acc)
        xp_ref[j, 0:SUBLANES, :] = xp_ref[j, q:q + SUBLANES, :]

    dt = _softplus(dt_ref[...] + dtb_ref[...])
    a = -jnp.exp(alog_ref[...])
    dta = dt * a
    ti = lax.broadcasted_iota(jnp.int32, (q, q), 0)
    si = lax.broadcasted_iota(jnp.int32, (q, q), 1)
    causal = si <= ti
    tril = jnp.where(causal, 1.0, 0.0).astype(F32)
    cs = jnp.dot(tril, dta, precision=lax.Precision.HIGHEST, preferred_element_type=F32)
    cs_t = lax.dot_general(dta, tril, (((0,), (1,)), ((), ())), precision=lax.Precision.HIGHEST,
                           preferred_element_type=F32)
    cs_last = cs[q - 1:q, :]
    e64s = e64_ref[...]
    dt_e = _split_dot_stacked(dt, e64s)
    ecs_e = _split_dot_stacked(jnp.exp(cs), e64s)
    toend_e = _split_dot_stacked(jnp.exp(cs_last - cs), e64s)
    dec_rows = _split_dot(jnp.broadcast_to(jnp.exp(cs_last), (8, n_heads)), e128_ref[...])

    lane = lax.broadcasted_iota(jnp.int32, (q, LANES), 1)
    left = lane < SSM_HEAD_DIM
    b_off = d_inner
    c_off = d_inner + SSM_GROUPS * D_STATE
    for g in range(SSM_GROUPS):
        gs = slice(g * gw, (g + 1) * gw)
        xs = xc_ref[:, gs]
        bg = xc_ref[:, b_off + g * D_STATE:b_off + (g + 1) * D_STATE].astype(BF16)
        cg = xc_ref[:, c_off + g * D_STATE:c_off + (g + 1) * D_STATE].astype(BF16)
        xdt = xs * dt_e[:, gs]
        cb = lax.dot_general(cg, bg, (((1,), (1,)), ((), ())), preferred_element_type=F32)
        y_parts = []
        for pr in range(hpg // 2):
            ms = []
            for hh in range(2):
                h = g * hpg + pr * 2 + hh
                seg = cs[:, h:h + 1] - cs_t[h:h + 1, :]
                dec = jnp.exp(jnp.where(causal, seg, NEG))
                ms.append((cb * dec).astype(BF16))
            xpair = xdt[:, pr * LANES:(pr + 1) * LANES]
            rhs = jnp.concatenate([jnp.where(left, xpair, 0.0), jnp.where(left, 0.0, xpair)], axis=0)
            y_parts.append(jnp.dot(jnp.concatenate(ms, axis=1), rhs.astype(BF16), preferred_element_type=F32))
        y = jnp.concatenate(y_parts, axis=1)
        st = st_ref[0, g * gw:(g + 1) * gw, :]
        y_off = lax.dot_general(cg, st.astype(BF16), (((1,), (1,)), ((), ())), preferred_element_type=F32)
        y = y + y_off * ecs_e[:, gs] + xs * dsk_ref[:, gs]
        upd = lax.dot_general((xdt * toend_e[:, gs]).astype(BF16), bg, (((0,), (0,)), ((), ())),
                              preferred_element_type=F32)
        for hh in range(hpg):
            h = g * hpg + hh
            rs = slice(hh * SSM_HEAD_DIM, (hh + 1) * SSM_HEAD_DIM)
            st_ref[0, g * gw + hh * SSM_HEAD_DIM:g * gw + (hh + 1) * SSM_HEAD_DIM, :] = (
                st[rs, :] * dec_rows[0:1, h * D_STATE:(h + 1) * D_STATE] + upd[rs, :])
        y = y * _silu(z_ref[:, gs])
        ms_ = jnp.mean(y * y, axis=-1, keepdims=True)
        y_ref[:, gs] = (y * lax.rsqrt(ms_ + EPS) * nw_ref[:, gs]).astype(y_ref.dtype)


def _ssd_prompt(xbc, z, dt, conv_w, conv_b, dt_bias, a_log, d_skip_e, norm_w, e64, e128, batch, seq):
    m, conv_dim = xbc.shape
    d_inner = z.shape[1]
    n_heads = dt.shape[1]
    nc = seq // SSD_CHUNK
    full = lambda a: pl.BlockSpec(a.shape, lambda b, c: (0,) * a.ndim)
    row = lambda n: pl.BlockSpec((SSD_CHUNK, n), lambda b, c: (b * nc + c, 0))
    return pl.pallas_call(
        functools.partial(_ssd_prompt_kernel, d_inner=d_inner),
        grid=(batch, nc),
        in_specs=[row(conv_dim), row(d_inner), row(n_heads), full(conv_w), full(conv_b), full(dt_bias),
                  full(a_log), full(d_skip_e), full(norm_w), full(e64), full(e128)],
        out_specs=[row(d_inner),
                   pl.BlockSpec((1, n_heads * SSM_HEAD_DIM, D_STATE), lambda b, c: (b, 0, 0))],
        out_shape=[jax.ShapeDtypeStruct((m, d_inner), BF16),
                   jax.ShapeDtypeStruct((batch, n_heads * SSM_HEAD_DIM, D_STATE), F32)],
        scratch_shapes=[pltpu.VMEM((conv_dim // LANES, SSD_CHUNK + SUBLANES, LANES), F32),
                        pltpu.VMEM((SSD_CHUNK, conv_dim), F32)],
        compiler_params=_cparams(("arbitrary", "arbitrary")),
        name="ssd_prompt",
    )(xbc, z, dt, conv_w, conv_b, dt_bias, a_log, d_skip_e, norm_w, e64, e128)


def _ssd_step_kernel(xbc_ref, z_ref, dt_ref, cst_ref, st_ref, cw_ref, cb_ref, dtb_ref, alog_ref, dsk_ref,
                     nw_ref, e64_ref, e128_ref, y_ref, cst_out_ref, st_out_ref, *, d_inner):
    n_heads = dt_ref.shape[2]
    hpg = n_heads // SSM_GROUPS
    gw = hpg * SSM_HEAD_DIM
    a = -jnp.exp(alog_ref[...])
    row0 = lax.broadcasted_iota(jnp.int32, (SUBLANES, D_STATE), 0) == 0
    b_off = d_inner
    c_off = d_inner + SSM_GROUPS * D_STATE
    for r in range(xbc_ref.shape[0]):
        x_new = xbc_ref[r]
        prev = cst_ref[r]
        acc = cb_ref[...] + cw_ref[CONV_W - 1:CONV_W, :] * x_new
        for k in range(CONV_W - 1):
            acc = acc + cw_ref[k:k + 1, :] * prev[k:k + 1, :]
        xc = _silu(acc)
        cst_out_ref[r, 0:CONV_W - 2, :] = prev[1:CONV_W - 1, :]
        cst_out_ref[r, CONV_W - 2:CONV_W - 1, :] = x_new

        dt = _softplus(dt_ref[r] + dtb_ref[...])
        dec = jnp.exp(dt * a)
        dt_e = _split_dot(jnp.broadcast_to(dt, (SUBLANES, n_heads)), e64_ref[...])[0:1, :]
        dec_rows = _split_dot(jnp.broadcast_to(dec, (SUBLANES, n_heads)), e128_ref[...])
        for g in range(SSM_GROUPS):
            gs = slice(g * gw, (g + 1) * gw)
            xs = xc[:, gs]
            xdt8 = jnp.broadcast_to(xs * dt_e[:, gs], (SUBLANES, gw)).astype(BF16)
            bg = xc[:, b_off + g * D_STATE:b_off + (g + 1) * D_STATE]
            cg = xc[:, c_off + g * D_STATE:c_off + (g + 1) * D_STATE]
            b8 = jnp.where(row0, jnp.broadcast_to(bg, (SUBLANES, D_STATE)), 0.0).astype(BF16)
            c8 = jnp.broadcast_to(cg, (SUBLANES, D_STATE)).astype(BF16)
            upd = lax.dot_general(xdt8, b8, (((0,), (0,)), ((), ())), preferred_element_type=F32)
            news = []
            for hh in range(hpg):
                h = g * hpg + hh
                rs = slice(g * gw + hh * SSM_HEAD_DIM, g * gw + (hh + 1) * SSM_HEAD_DIM)
                new = (st_ref[r, rs, :] * dec_rows[0:1, h * D_STATE:(h + 1) * D_STATE]
                       + upd[hh * SSM_HEAD_DIM:(hh + 1) * SSM_HEAD_DIM, :])
                st_out_ref[r, rs, :] = new
                news.append(new)
            st_new = jnp.concatenate(news, axis=0).astype(BF16)
            y = lax.dot_general(c8, st_new, (((1,), (1,)), ((), ())), preferred_element_type=F32)[0:1, :]
            y = y + xs * dsk_ref[:, gs]
            y = y * _silu(z_ref[r, :, gs])
            ms_ = jnp.mean(y * y, axis=-1, keepdims=True)
            y_ref[r, :, gs] = (y * lax.rsqrt(ms_ + EPS) * nw_ref[:, gs]).astype(y_ref.dtype)


def _ssd_step(xbc, z, dt, conv_state, ssm_state, conv_w, conv_b, dt_bias, a_log, d_skip_e, norm_w, e64, e128):
    bsz, conv_dim = xbc.shape
    d_inner = z.shape[1]
    n_heads = dt.shape[1]
    rows = n_heads * SSM_HEAD_DIM
    full = lambda a: pl.BlockSpec(a.shape, lambda b: (0,) * a.ndim)
    per_b = lambda *s: pl.BlockSpec((STEP_SEQS,) + s, lambda b: (b,) + (0,) * len(s))
    return pl.pallas_call(
        functools.partial(_ssd_step_kernel, d_inner=d_inner),
        grid=(bsz // STEP_SEQS,),
        in_specs=[per_b(1, conv_dim), per_b(1, d_inner), per_b(1, n_heads), per_b(CONV_W - 1, conv_dim),
                  per_b(rows, D_STATE), full(conv_w), full(conv_b), full(dt_bias), full(a_log),
                  full(d_skip_e), full(norm_w), full(e64), full(e128)],
        out_specs=[per_b(1, d_inner), per_b(CONV_W - 1, conv_dim), per_b(rows, D_STATE)],
        out_shape=[jax.ShapeDtypeStruct((bsz, 1, d_inner), BF16),
                   jax.ShapeDtypeStruct((bsz, CONV_W - 1, conv_dim), F32),
                   jax.ShapeDtypeStruct((bsz, rows, D_STATE), F32)],
        compiler_params=_cparams(("arbitrary",)),
        name="ssd_step",
    )(xbc.reshape(bsz, 1, conv_dim), z.reshape(bsz, 1, d_inner), dt.reshape(bsz, 1, n_heads), conv_state,
      ssm_state.reshape(bsz, rows, D_STATE), conv_w, conv_b, dt_bias, a_log, d_skip_e, norm_w, e64, e128)


def _attn_kernel(pt_ref, slopes_ref, lam_ref,
                 q_ref, k_ref, v_ref, g_ref, sw_ref,
                 kc_ref, vc_ref, qd_ref, knd_ref, vnd_ref, gd_ref, slope_ref, e_ref,
                 o_ref, od_ref,
                 kb_ref, vb_ref, qs_ref, m_ref, a_ref,
                 kbuf_ref, vbuf_ref, ksem_ref, vsem_ref, qb_ref, mk_ref, md_ref, ld_ref, accd_ref,
                 *, out_scale, n_pages, page_size):
    t = ATT_BLOCK
    npp = PAGES_PER_STEP
    nslot = DECODE_SLOTS
    seq = q_ref.shape[0]
    nblk = seq // t
    n_hc, width = qb_ref.shape
    n_heads = n_hc // 2
    steps_per_seq = n_pages // npp
    g_idx = pl.program_id(0) * pl.num_programs(1) + pl.program_id(1)
    n_dsteps = pl.num_programs(0) * pl.num_programs(1) * nblk
    slope = slopes_ref[pl.program_id(1)] * LOG2E
    lam = lam_ref[0]

    def page_copies(dstep):
        slot = dstep % nslot
        bs = dstep // steps_per_seq
        s = dstep % steps_per_seq
        cps = []
        for i in range(npp):
            page = pt_ref[bs, s * npp + i]
            cps.append(pltpu.make_async_copy(kc_ref.at[page], kbuf_ref.at[slot, i], ksem_ref.at[slot]))
            cps.append(pltpu.make_async_copy(vc_ref.at[page], vbuf_ref.at[slot, i], vsem_ref.at[slot]))
        return cps

    def start_pages(dstep):
        for cp in page_copies(dstep):
            cp.start()

    def wait_pages(dstep):
        for cp in page_copies(dstep):
            cp.wait()

    @pl.when(g_idx == 0)
    def _():
        for d0 in range(nslot - 1):
            start_pages(d0)
        r = lax.broadcasted_iota(jnp.int32, (n_hc, page_size * n_heads), 0)
        ln = lax.broadcasted_iota(jnp.int32, (n_hc, page_size * n_heads), 1)
        mk_ref[...] = jnp.where(ln % n_heads == r % n_heads, 1.0, 0.0).astype(BF16)
        vb_ref[:, ATT_V_DIM:2 * ATT_V_DIM] = jnp.ones((seq, ATT_V_DIM), BF16)

    def decode_init(dstep):
        @pl.when(dstep % steps_per_seq == 0)
        def _():
            r = lax.broadcasted_iota(jnp.int32, (n_hc, width), 0)
            ln = lax.broadcasted_iota(jnp.int32, (n_hc, width), 1)
            hc = 2 * (r % n_heads) + r // n_heads
            sel = ln // ATT_HEAD_DIM == hc
            qrow = jnp.broadcast_to(qd_ref[0].astype(F32), (n_hc, width))
            qblk = jnp.where(sel, qrow, 0.0)
            qb_ref[...] = qblk.astype(BF16)
            s_new = jnp.sum(qblk * knd_ref[0], axis=1, keepdims=True)
            md_ref[...] = jnp.broadcast_to(s_new, (n_hc, LANES))
            ld_ref[...] = jnp.ones((n_hc, LANES), F32)
            vn = vnd_ref[0]
            accd_ref[...] = jnp.concatenate([vn, vn], axis=0)

    def decode_main(dstep):
        s_idx = dstep % steps_per_seq
        slot = dstep % nslot
        qblk = qb_ref[...]
        srow = slope_ref[...]
        kofs = lax.broadcasted_iota(jnp.int32, (1, page_size), 1)
        s_parts = []
        for i in range(npp):
            kt = kbuf_ref[slot, i].reshape(width, page_size).astype(BF16)
            s = jnp.dot(qblk, kt, preferred_element_type=F32)
            kpos = (s_idx * npp + i) * page_size + kofs
            dist = (n_pages * page_size - kpos).astype(F32)
            s_parts.append(s - srow * dist)
        m_prev = md_ref[...]
        m_cur = s_parts[0].max(axis=1, keepdims=True)
        for i in range(1, npp):
            m_cur = jnp.maximum(m_cur, s_parts[i].max(axis=1, keepdims=True))
        m_new = jnp.maximum(m_prev, m_cur)
        alpha = jnp.exp2(m_prev - m_new)
        l_new = alpha * ld_ref[...]
        ps = []
        for i in range(npp):
            p = jnp.exp2(s_parts[i] - m_new)
            l_new = l_new + jnp.sum(p, axis=1, keepdims=True)
            ps.append(p.astype(BF16))
        pe_all = jnp.dot(jnp.concatenate(ps, axis=0), e_ref[...], preferred_element_type=F32).astype(BF16)
        pv = None
        for i in range(npp):
            pe = pe_all[i * n_hc:(i + 1) * n_hc, :] * mk_ref[...]
            v2 = vbuf_ref[slot, i].reshape(page_size * n_heads, ATT_V_DIM).astype(BF16)
            d = jnp.dot(pe, v2, preferred_element_type=F32)
            pv = d if pv is None else pv + d
        accd_ref[...] = alpha * accd_ref[...] + pv
        ld_ref[...] = l_new
        md_ref[...] = m_new

    def decode_final(dstep):
        @pl.when(dstep % steps_per_seq == steps_per_seq - 1)
        def _():
            o = (accd_ref[0:n_heads, :] / ld_ref[0:n_heads, :]
                 - lam * (accd_ref[n_heads:n_hc, :] / ld_ref[n_heads:n_hc, :]))
            ms = jnp.mean(o * o, axis=-1, keepdims=True)
            o = o * lax.rsqrt(ms + EPS) * sw_ref[...] * out_scale
            od_ref[0] = (o * _silu(gd_ref[0])).astype(od_ref.dtype)

    for j in range(nblk):
        kb_ref[j] = k_ref[0, 0, :, :, j * t:(j + 1) * t].reshape(2 * ATT_HEAD_DIM, t).astype(BF16)
    vb_ref[:, 0:ATT_V_DIM] = v_ref[...].astype(BF16)
    first = lax.broadcasted_iota(jnp.int32, (t, LANES), 1) < ATT_HEAD_DIM
    kidx = lax.broadcasted_iota(jnp.int32, (1, t), 1).astype(F32)

    def q_block(qi, carry):
        dstep = g_idx * nblk + qi
        nxt = dstep + (nslot - 1)

        @pl.when(nxt < n_dsteps)
        def _():
            start_pages(nxt)

        wait_pages(dstep)
        decode_init(dstep)

        q0 = pl.multiple_of(qi * t, t)
        qp = q_ref[pl.ds(q0, t), :].astype(F32)
        qs_ref[0:t, :] = jnp.where(first, qp, 0.0).astype(BF16)
        qs_ref[t:2 * t, :] = jnp.where(first, 0.0, qp).astype(BF16)

        ri = lax.broadcasted_iota(jnp.int32, (2 * t, t), 0)
        ci = lax.broadcasted_iota(jnp.int32, (2 * t, t), 1)
        causal = ci <= jnp.where(ri >= t, ri - t, ri)
        s = jnp.dot(qs_ref[...], kb_ref[qi], preferred_element_type=F32) + slope * kidx
        s = jnp.where(causal, s, NEG)
        m = jnp.max(s, axis=1, keepdims=True)
        p = jnp.exp2(s - m)
        m_ref[...] = jnp.broadcast_to(m, (2 * t, LANES))
        a_ref[...] = jnp.dot(p.astype(BF16), vb_ref[pl.ds(q0, t), :], preferred_element_type=F32)

        decode_main(dstep)
        decode_final(dstep)

        def kv_block(j, carry2):
            k0 = pl.multiple_of(j * t, t)
            bias = slope * (kidx + (k0 - q0).astype(F32))
            s = jnp.dot(qs_ref[...], kb_ref[j], preferred_element_type=F32) + bias
            m_prev = m_ref[...]
            m_new = jnp.maximum(m_prev, jnp.max(s, axis=1, keepdims=True))
            alpha = jnp.exp2(m_prev - m_new)
            p = jnp.exp2(s - jnp.concatenate([m_new] * (t // LANES), axis=1))
            a_ref[...] = (jnp.concatenate([alpha, alpha], axis=1) * a_ref[...]
                          + jnp.dot(p.astype(BF16), vb_ref[pl.ds(k0, t), :], preferred_element_type=F32))
            m_ref[...] = m_new
            return carry2

        lax.fori_loop(0, qi, kv_block, 0)

        o = (a_ref[0:t, 0:ATT_V_DIM] / a_ref[0:t, ATT_V_DIM:]
             - lam * (a_ref[t:2 * t, 0:ATT_V_DIM] / a_ref[t:2 * t, ATT_V_DIM:]))
        ms = jnp.mean(o * o, axis=-1, keepdims=True)
        o = o * lax.rsqrt(ms + EPS) * sw_ref[...] * out_scale
        o_ref[pl.ds(q0, t), :] = (o * _silu(g_ref[pl.ds(q0, t), :])).astype(o_ref.dtype)
        return carry

    lax.fori_loop(0, nblk, q_block, 0)


def _attn_fused(page_table, slopes, lam, q, kt, v, g, subln_w, kc, vc, qd, knd, vnd, gd, slope_rows, e_tok,
                batch, seq, out_scale):
    m, width = q.shape
    n_heads = width // ATT_V_DIM
    n_hc = 2 * n_heads
    t = ATT_BLOCK
    npp = PAGES_PER_STEP
    bsz, n_pages = page_table.shape
    page_size = kc.shape[-1]
    nblk = seq // t
    assert bsz * (n_pages // npp) == batch * n_heads * nblk, "one decode page group per prompt q tile"
    gps = (n_pages // npp) // nblk
    assert gps * nblk * npp == n_pages
    blk = pl.BlockSpec((seq, ATT_V_DIM), lambda b, h, *_: (b, h))
    dseq = lambda b, h: (b * n_heads + h) // gps
    rowd = pl.BlockSpec((1, 1, width), lambda b, h, *_: (dseq(b, h), 0, 0))
    headd = pl.BlockSpec((1, n_heads, ATT_V_DIM), lambda b, h, *_: (dseq(b, h), 0, 0))
    const = lambda shape: pl.BlockSpec(shape, lambda b, h, *_: (0,) * len(shape))
    grid_spec = pltpu.PrefetchScalarGridSpec(
        num_scalar_prefetch=3,
        grid=(batch, n_heads),
        in_specs=[blk,
                  pl.BlockSpec((1, 1, 2, ATT_HEAD_DIM, seq), lambda b, h, *_: (b, h, 0, 0, 0)),
                  blk, blk, const((1, ATT_V_DIM)),
                  pl.BlockSpec(memory_space=pl.ANY), pl.BlockSpec(memory_space=pl.ANY),
                  rowd, rowd, headd, headd, const((n_hc, LANES)), const((page_size, page_size * n_heads))],
        out_specs=[blk, headd],
        scratch_shapes=[pltpu.VMEM((nblk, 2 * ATT_HEAD_DIM, t), BF16), pltpu.VMEM((seq, 2 * ATT_V_DIM), BF16),
                        pltpu.VMEM((2 * t, LANES), BF16),
                        pltpu.VMEM((2 * t, LANES), F32), pltpu.VMEM((2 * t, 2 * ATT_V_DIM), F32),
                        pltpu.VMEM((DECODE_SLOTS, npp, n_heads, 2, ATT_HEAD_DIM, page_size), F32),
                        pltpu.VMEM((DECODE_SLOTS, npp, page_size, n_heads, ATT_V_DIM), F32),
                        pltpu.SemaphoreType.DMA((DECODE_SLOTS,)), pltpu.SemaphoreType.DMA((DECODE_SLOTS,)),
                        pltpu.VMEM((n_hc, width), BF16), pltpu.VMEM((n_hc, page_size * n_heads), BF16),
                        pltpu.VMEM((n_hc, LANES), F32), pltpu.VMEM((n_hc, LANES), F32),
                        pltpu.VMEM((n_hc, ATT_V_DIM), F32)],
    )
    o, od = pl.pallas_call(
        functools.partial(_attn_kernel, out_scale=out_scale, n_pages=n_pages, page_size=page_size),
        grid_spec=grid_spec,
        out_shape=[jax.ShapeDtypeStruct((m, width), BF16), jax.ShapeDtypeStruct((bsz, n_heads, ATT_V_DIM), BF16)],
        compiler_params=_cparams(("arbitrary", "arbitrary")),
        name="attn_fused",
    )(page_table, slopes, lam, q, kt, v, g, subln_w, kc, vc,
      qd.reshape(bsz, 1, width), knd.reshape(bsz, 1, width), vnd.reshape(bsz, n_heads, ATT_V_DIM),
      gd.reshape(bsz, n_heads, ATT_V_DIM), slope_rows, e_tok)
    return o, od.reshape(bsz, width)


def _expand_matrix(n_rows, rep):
    r = lax.broadcasted_iota(jnp.int32, (n_rows, n_rows * rep), 0)
    c = lax.broadcasted_iota(jnp.int32, (n_rows, n_rows * rep), 1)
    return (c // rep == r).astype(BF16)


def kernel(x_prompt, x_sample, state_conv, state_ssm, cache_k, cache_v, page_table, c_prompt, c_sample,
           ada_w, ada_b, ln_w, ssm_w_in, ssm_conv_w, ssm_conv_b, ssm_dt_bias, ssm_a_log, ssm_d,
           ssm_norm_w, ssm_w_out, att_w_in, att_q_norm, att_k_norm, att_lambda, att_subln_w, att_w_out):
    batch, seq, d = x_prompt.shape
    dec_batch = x_sample.shape[0]
    n_ssm_heads = ssm_a_log.shape[1]
    d_inner = n_ssm_heads * SSM_HEAD_DIM
    conv_dim = ssm_conv_w.shape[2]
    att_width = att_w_out.shape[1]
    n_att_heads = att_width // ATT_V_DIM
    m = batch * seq
    tm = 512
    tm_out = 1024
    tiles_per_seq = seq // tm

    mod = _adaln(jnp.concatenate([c_prompt, c_sample], axis=0), ada_w, ada_b)

    def mods(i):
        parts = [mod[i, :, k * d:(k + 1) * d] for k in range(3)]
        return ([p[:batch].reshape(batch, 1, d) for p in parts],
                [p[batch:].reshape(1, dec_batch, d) for p in parts])

    xp = x_prompt.reshape(m, d)
    xs = x_sample.reshape(dec_batch, d)

    (sh_p, sc_p, gt_p), (sh_s, sc_s, gt_s) = mods(0)
    w_in = ssm_w_in[0].astype(BF16)
    wz, wx, wd = w_in[:, :d_inner], w_in[:, d_inner:d_inner + conv_dim], w_in[:, d_inner + conv_dim:]
    lnw0 = ln_w[0].reshape(1, d)
    conv_w, conv_b = ssm_conv_w[0], ssm_conv_b[0].reshape(1, conv_dim)
    dt_bias, a_log = ssm_dt_bias[0].reshape(1, -1), ssm_a_log[0].reshape(1, -1)
    d_skip_e = jnp.repeat(ssm_d[0], SSM_HEAD_DIM).reshape(1, d_inner)
    norm_w = ssm_norm_w[0].reshape(1, d_inner)
    w_out0 = ssm_w_out[0].astype(BF16)
    e64 = _expand_matrix(n_ssm_heads, SSM_HEAD_DIM)
    e128 = _expand_matrix(n_ssm_heads, D_STATE)

    z_p, xbc_p, dt_p = _inproj_ssm(xp, lnw0, sc_p, sh_p, wz, wx, wd, tm, tiles_per_seq)
    yn_p, ssm_p = _ssd_prompt(xbc_p, z_p, dt_p, conv_w, conv_b, dt_bias, a_log, d_skip_e, norm_w,
                              jnp.concatenate([e64, e64], axis=0), e128, batch, seq)
    y1_p = _outproj(yn_p, w_out0, xp, gt_p, tm_out, seq // tm_out)
    conv_p = xbc_p.reshape(batch, seq, conv_dim)[:, seq - (CONV_W - 1):, :]

    z_s, xbc_s, dt_s = _inproj_ssm(xs, lnw0, sc_s, sh_s, wz, wx, wd, dec_batch, 1)
    yn_s, conv_s, ssm_s = _ssd_step(xbc_s, z_s, dt_s, state_conv[0], state_ssm[0], conv_w, conv_b, dt_bias,
                                    a_log, d_skip_e, norm_w, e64, e128)
    y1_s = _outproj(yn_s.reshape(dec_batch, d_inner), w_out0, xs, gt_s, dec_batch, 1)

    (sh_p, sc_p, gt_p), (sh_s, sc_s, gt_s) = mods(1)
    lam_init = 0.8 - 0.6 * math.exp(-0.3 * 1)
    lmb = att_lambda[0].astype(F32)
    lam = (jnp.exp(jnp.sum(lmb[0] * lmb[1])) - jnp.exp(jnp.sum(lmb[2] * lmb[3])) + lam_init).reshape(1)
    slopes = jnp.exp2(-8.0 * jnp.arange(1, n_att_heads + 1, dtype=F32) / n_att_heads)
    wa = att_w_in[0].astype(BF16)
    wq, wk, wv, wg = (wa[:, i * att_width:(i + 1) * att_width] for i in range(4))
    wkt = wk.T
    lnw1 = ln_w[1].reshape(1, d)
    n_rep = att_width // ATT_HEAD_DIM
    qn = jnp.tile(att_q_norm[0], n_rep).reshape(1, att_width)
    gi = lax.broadcasted_iota(jnp.int32, (2 * LANES, 2 * LANES), 0) // ATT_HEAD_DIM
    gj = lax.broadcasted_iota(jnp.int32, (2 * LANES, 2 * LANES), 1) // ATT_HEAD_DIM
    gsum = (gi == gj).astype(BF16)
    subln_w = att_subln_w[0].reshape(1, ATT_V_DIM)
    w_out1 = att_w_out[0].astype(BF16)
    out_scale = 1.0 - lam_init

    def kn_rep(cols):
        return jnp.broadcast_to(att_k_norm[0][:, None], (ATT_HEAD_DIM, cols))

    q_p, kt_p, v_p, g_p = _inproj_att(y1_p, lnw1, sc_p, sh_p, wq, wkt, wv, wg, qn, kn_rep(tm), gsum, tm,
                                      tiles_per_seq)
    kt_p = kt_p.reshape(batch, n_att_heads, 2, ATT_HEAD_DIM, seq)

    q_s, kt_s, v_s, g_s = _inproj_att(y1_s, lnw1, sc_s, sh_s, wq, wkt, wv, wg, qn, kn_rep(dec_batch), gsum,
                                      dec_batch, 1)
    k_s = kt_s[0].T
    page_size = cache_k.shape[2]
    n_hc = 2 * n_att_heads
    slope_rows = jnp.broadcast_to((jnp.tile(slopes, 2) * LOG2E)[:, None], (n_hc, LANES))
    ei = lax.broadcasted_iota(jnp.int32, (page_size, page_size * n_att_heads), 0)
    ej = lax.broadcasted_iota(jnp.int32, (page_size, page_size * n_att_heads), 1) // n_att_heads
    e_tok = (ei == ej).astype(BF16)
    kc = jnp.transpose(cache_k[0], (0, 2, 3, 4, 1))
    o_p, o_s = _attn_fused(page_table, slopes, lam, q_p, kt_p, v_p, g_p, subln_w, kc, cache_v[0],
                           q_s, k_s, v_s, g_s, slope_rows, e_tok, batch, seq, out_scale)
    y2_p = _outproj(o_p, w_out1, y1_p, gt_p, tm_out, seq // tm_out)
    y2_s = _outproj(o_s, w_out1, y1_s, gt_s, dec_batch, 1)
    k_p = jnp.transpose(kt_p, (0, 4, 1, 2, 3))

    return (y2_p.reshape(batch, seq, d),
            y2_s.reshape(dec_batch, 1, d),
            conv_p[None],
            ssm_p.reshape(1, batch, n_ssm_heads, SSM_HEAD_DIM, D_STATE),
            k_p[None],
            v_p.reshape(1, batch, seq, n_att_heads, ATT_V_DIM),
            conv_s[None],
            ssm_s.reshape(1, dec_batch, n_ssm_heads, SSM_HEAD_DIM, D_STATE),
            k_s.reshape(1, dec_batch, 1, n_att_heads, 2, ATT_HEAD_DIM),
            v_s.reshape(1, dec_batch, 1, n_att_heads, ATT_V_DIM))
```

```python
import functools
import math

import jax
import jax.numpy as jnp
from jax import lax
from jax.experimental import pallas as pl
from jax.experimental.pallas import tpu as pltpu

F32 = jnp.float32
BF16 = jnp.bfloat16
EPS = 1e-6
NEG = -1e30

LANES = 128
SUBLANES = 8
SSD_CHUNK = 128
STEP_SEQS = 8
SSM_HEAD_DIM = 64
SSM_GROUPS = 8
D_STATE = 128
CONV_W = 4
ATT_HEAD_DIM = 64
ATT_V_DIM = 128
ATT_BLOCK = 512
PAGES_PER_STEP = 4
DECODE_SLOTS = 3
LOG2E = 1.4426950408889634
VMEM_LIMIT = 56 * 1024 * 1024


def _cparams(sem):
    return pltpu.CompilerParams(dimension_semantics=sem, vmem_limit_bytes=VMEM_LIMIT)


def _silu(x):
    return x / (1.0 + jnp.exp2(x * -LOG2E))


def _softplus(x):
    return jnp.maximum(x, 0.0) + jnp.log(1.0 + jnp.exp(-jnp.abs(x)))


def _split_dot(v, e):
    hi = v.astype(BF16)
    lo = (v - hi.astype(F32)).astype(BF16)
    return (jnp.dot(hi, e, preferred_element_type=F32)
            + jnp.dot(lo, e, preferred_element_type=F32))


def _split_dot_stacked(v, e2):
    hi = v.astype(BF16)
    lo = (v - hi.astype(F32)).astype(BF16)
    return jnp.dot(jnp.concatenate([hi, lo], axis=1), e2, preferred_element_type=F32)


def _norm_mod(x, lnw, scale, shift):
    ms = jnp.mean(x * x, axis=-1, keepdims=True)
    y = x * lax.rsqrt(ms + EPS) * lnw
    return y * (1.0 + scale) + shift


def _adaln_kernel(c_ref, w_ref, b_ref, o_ref):
    s = _silu(c_ref[...])
    o_ref[0] = jnp.dot(s.astype(BF16), w_ref[0].astype(BF16), preferred_element_type=F32) + b_ref[0]


def _adaln(c_all, ada_w, ada_b):
    depth, d, d3 = ada_w.shape
    r = c_all.shape[0]
    return pl.pallas_call(
        _adaln_kernel,
        grid=(depth, d3 // d),
        in_specs=[pl.BlockSpec((r, d), lambda i, j: (0, 0)),
                  pl.BlockSpec((1, d, d), lambda i, j: (i, 0, j)),
                  pl.BlockSpec((1, 1, d), lambda i, j: (i, 0, j))],
        out_specs=pl.BlockSpec((1, r, d), lambda i, j: (i, 0, j)),
        out_shape=jax.ShapeDtypeStruct((depth, r, d3), F32),
        compiler_params=_cparams(("arbitrary", "arbitrary")),
        name="adaln",
    )(c_all, ada_w, ada_b.reshape(depth, 1, d3))


N_CHUNK = 512


def _inproj_ssm_kernel(x_ref, lnw_ref, sc_ref, sh_ref, wz_ref, wx_ref, wd_ref, z_ref, xbc_ref, dt_ref):
    h = _norm_mod(x_ref[...], lnw_ref[...], sc_ref[0], sh_ref[0]).astype(BF16)
    for w_ref, o_ref in ((wz_ref, z_ref), (wx_ref, xbc_ref)):
        for j in range(0, w_ref.shape[1], N_CHUNK):
            o_ref[:, j:j + N_CHUNK] = jnp.dot(h, w_ref[:, j:j + N_CHUNK], preferred_element_type=F32)
    dt_ref[...] = jnp.dot(h, wd_ref[...], preferred_element_type=F32)


def _mod_specs(mod_rows, d, tm, tiles_per_seq):
    if mod_rows == 1:
        return pl.BlockSpec((1, 1, d), lambda i: (i // tiles_per_seq, 0, 0))
    return pl.BlockSpec((1, tm, d), lambda i: (0, i, 0))


def _inproj_ssm(x, lnw, scale, shift, wz, wx, wd, tm, tiles_per_seq):
    m, d = x.shape
    full = lambda a: pl.BlockSpec(a.shape, lambda i: (0,) * a.ndim, pipeline_mode=pl.Buffered(1))
    row = lambda n: pl.BlockSpec((tm, n), lambda i: (i, 0))
    mod = _mod_specs(scale.shape[1], d, tm, tiles_per_seq)
    return pl.pallas_call(
        _inproj_ssm_kernel,
        grid=(m // tm,),
        in_specs=[row(d), full(lnw), mod, mod, full(wz), full(wx), full(wd)],
        out_specs=[row(wz.shape[1]), row(wx.shape[1]), row(wd.shape[1])],
        out_shape=[jax.ShapeDtypeStruct((m, wz.shape[1]), F32),
                   jax.ShapeDtypeStruct((m, wx.shape[1]), F32),
                   jax.ShapeDtypeStruct((m, wd.shape[1]), F32)],
        compiler_params=_cparams(("arbitrary",)),
        name="inproj_ssm",
    )(x, lnw, scale, shift, wz, wx, wd)


def _inproj_att_kernel(x_ref, lnw_ref, sc_ref, sh_ref, wq_ref, wkt_ref, wv_ref, wg_ref, qn_ref, kn_ref,
                       gsum_ref, q_ref, kt_ref, v_ref, g_ref, *, q_scale):
    h = _norm_mod(x_ref[...], lnw_ref[...], sc_ref[0], sh_ref[0]).astype(BF16)
    tm = h.shape[0]
    gsum = gsum_ref[...]
    width = gsum.shape[0]
    inv_hd = 1.0 / ATT_HEAD_DIM
    for j in range(0, wq_ref.shape[1], N_CHUNK):
        qc = jnp.dot(h, wq_ref[:, j:j + N_CHUNK], preferred_element_type=F32)
        for i in range(0, N_CHUNK, width):
            t = qc[:, i:i + width]
            ms = _split_dot(t * t, gsum) * inv_hd
            qn = t * lax.rsqrt(ms + EPS) * qn_ref[:, j + i:j + i + width]
            q_ref[:, j + i:j + i + width] = (qn * q_scale).astype(q_ref.dtype)
        v_ref[:, j:j + N_CHUNK] = jnp.dot(h, wv_ref[:, j:j + N_CHUNK], preferred_element_type=F32)
        g_ref[:, j:j + N_CHUNK] = jnp.dot(h, wg_ref[:, j:j + N_CHUNK], preferred_element_type=F32)
        kc = lax.dot_general(wkt_ref[j:j + N_CHUNK, :], h, (((1,), (1,)), ((), ())), preferred_element_type=F32)
        k3 = kc.reshape(N_CHUNK // ATT_HEAD_DIM, ATT_HEAD_DIM, tm)
        ms = jnp.mean(k3 * k3, axis=1, keepdims=True)
        kt_ref[0, j:j + N_CHUNK, :] = (k3 * lax.rsqrt(ms + EPS) * kn_ref[...][None]).reshape(N_CHUNK, tm)


def _inproj_att(x, lnw, scale, shift, wq, wkt, wv, wg, qn, kn_rep, gsum, tm, tiles_per_seq):
    m, d = x.shape
    n = wq.shape[1]
    seq = tm * tiles_per_seq
    full = lambda a: pl.BlockSpec(a.shape, lambda i: (0,) * a.ndim, pipeline_mode=pl.Buffered(1))
    row = lambda c: pl.BlockSpec((tm, c), lambda i: (i, 0))
    mod = _mod_specs(scale.shape[1], d, tm, tiles_per_seq)
    return pl.pallas_call(
        functools.partial(_inproj_att_kernel, q_scale=ATT_HEAD_DIM ** -0.5 * LOG2E),
        grid=(m // tm,),
        in_specs=[row(d), full(lnw), mod, mod, full(wq), full(wkt), full(wv), full(wg), full(qn), full(kn_rep),
                  full(gsum)],
        out_specs=[row(n), pl.BlockSpec((1, n, tm), lambda i: (i // tiles_per_seq, 0, i % tiles_per_seq)),
                   row(n), row(n)],
        out_shape=[jax.ShapeDtypeStruct((m, n), BF16), jax.ShapeDtypeStruct((m // seq, n, seq), F32),
                   jax.ShapeDtypeStruct((m, n), F32), jax.ShapeDtypeStruct((m, n), F32)],
        compiler_params=_cparams(("arbitrary",)),
        name="inproj_att",
    )(x, lnw, scale, shift, wq, wkt, wv, wg, qn, kn_rep, gsum)


def _outproj_kernel(a_ref, w_ref, x_ref, gate_ref, o_ref):
    y = jnp.dot(a_ref[...], w_ref[...], preferred_element_type=F32)
    o_ref[...] = x_ref[...] + gate_ref[0] * y


def _outproj(a, w, xres, gate, tm, tiles_per_seq):
    m, k = a.shape
    d = w.shape[1]
    return pl.pallas_call(
        _outproj_kernel,
        grid=(m // tm,),
        in_specs=[pl.BlockSpec((tm, k), lambda i: (i, 0)),
                  pl.BlockSpec((k, d), lambda i: (0, 0)),
                  pl.BlockSpec((tm, d), lambda i: (i, 0)),
                  _mod_specs(gate.shape[1], d, tm, tiles_per_seq)],
        out_specs=pl.BlockSpec((tm, d), lambda i: (i, 0)),
        out_shape=jax.ShapeDtypeStruct((m, d), F32),
        compiler_params=_cparams(("arbitrary",)),
        name="outproj",
    )(a, w, xres, gate)


def _ssd_prompt_kernel(xbc_ref, z_ref, dt_ref, cw_ref, cb_ref, dtb_ref, alog_ref, dsk_ref, nw_ref,
                       e64_ref, e128_ref, y_ref, st_ref, xp_ref, xc_ref, *, d_inner):
    q = SSD_CHUNK
    conv_dim = xbc_ref.shape[1]
    n_heads = dt_ref.shape[1]
    hpg = n_heads // SSM_GROUPS
    gw = hpg * SSM_HEAD_DIM
    c_idx = pl.program_id(1)

    @pl.when(c_idx == 0)
    def _():
        xp_ref[:, 0:SUBLANES, :] = jnp.zeros((conv_dim // LANES, SUBLANES, LANES), F32)
        st_ref[...] = jnp.zeros(st_ref.shape, F32)

    for j in range(conv_dim // LANES):
        cs_ = slice(j * LANES, (j + 1) * LANES)
        xp_ref[j, SUBLANES:SUBLANES + q, :] = xbc_ref[:, cs_]
        acc = cb_ref[:, cs_] + cw_ref[CONV_W - 1:CONV_W, cs_] * xp_ref[j, SUBLANES:SUBLANES + q, :]
        for k in range(CONV_W - 1):
            r0 = SUBLANES - (CONV_W - 1) + k
            acc = acc + cw_ref[k:k + 1, cs_] * xp_ref[j, r0:r0 + q, :]
        xc_ref[:, cs_] = _silu(acc)
        xp_ref[j, 0:SUBLANES, :] = xp_ref[j, q:q + SUBLANES, :]

    dt = _softplus(dt_ref[...] + dtb_ref[...])
    a = -jnp.exp(alog_ref[...])
    dta = dt * a
    ti = lax.broadcasted_iota(jnp.int32, (q, q), 0)
    si = lax.broadcasted_iota(jnp.int32, (q, q), 1)
    causal = si <= ti
    tril = jnp.where(causal, 1.0, 0.0).astype(F32)
    cs = jnp.dot(tril, dta, precision=lax.Precision.HIGHEST, preferred_element_type=F32)
    cs_t = lax.dot_general(dta, tril, (((0,), (1,)), ((), ())), precision=lax.Precision.HIGHEST,
                           preferred_element_type=F32)
    cs_last = cs[q - 1:q, :]
    e64s = e64_ref[...]
    dt_e = _split_dot_stacked(dt, e64s)
    ecs_e = _split_dot_stacked(jnp.exp(cs), e64s)
    toend_e = _split_dot_stacked(jnp.exp(cs_last - cs), e64s)
    dec_rows = _split_dot(jnp.broadcast_to(jnp.exp(cs_last), (8, n_heads)), e128_ref[...])

    lane = lax.broadcasted_iota(jnp.int32, (q, LANES), 1)
    left = lane < SSM_HEAD_DIM
    b_off = d_inner
    c_off = d_inner + SSM_GROUPS * D_STATE
    for g in range(SSM_GROUPS):
        gs = slice(g * gw, (g + 1) * gw)
        xs = xc_ref[:, gs]
        bg = xc_ref[:, b_off + g * D_STATE:b_off + (g + 1) * D_STATE].astype(BF16)
        cg = xc_ref[:, c_off + g * D_STATE:c_off + (g + 1) * D_STATE].astype(BF16)
        xdt = xs * dt_e[:, gs]
        cb = lax.dot_general(cg, bg, (((1,), (1,)), ((), ())), preferred_element_type=F32)
        y_parts = []
        for pr in range(hpg // 2):
            ms = []
            for hh in range(2):
                h = g * hpg + pr * 2 + hh
                seg = cs[:, h:h + 1] - cs_t[h:h + 1, :]
                dec = jnp.exp(jnp.where(causal, seg, NEG))
                ms.append((cb * dec).astype(BF16))
            xpair = xdt[:, pr * LANES:(pr + 1) * LANES]
            rhs = jnp.concatenate([jnp.where(left, xpair, 0.0), jnp.where(left, 0.0, xpair)], axis=0)
            y_parts.append(jnp.dot(jnp.concatenate(ms, axis=1), rhs.astype(BF16), preferred_element_type=F32))
        y = jnp.concatenate(y_parts, axis=1)
        st = st_ref[0, g * gw:(g + 1) * gw, :]
        y_off = lax.dot_general(cg, st.astype(BF16), (((1,), (1,)), ((), ())), preferred_element_type=F32)
        y = y + y_off * ecs_e[:, gs] + xs * dsk_ref[:, gs]
        upd = lax.dot_general((xdt * toend_e[:, gs]).astype(BF16), bg, (((0,), (0,)), ((), ())),
                              preferred_element_type=F32)
        for hh in range(hpg):
            h = g * hpg + hh
            rs = slice(hh * SSM_HEAD_DIM, (hh + 1) * SSM_HEAD_DIM)
            st_ref[0, g * gw + hh * SSM_HEAD_DIM:g * gw + (hh + 1) * SSM_HEAD_DIM, :] = (
                st[rs, :] * dec_rows[0:1, h * D_STATE:(h + 1) * D_STATE] + upd[rs, :])
        y = y * _silu(z_ref[:, gs])
        ms_ = jnp.mean(y * y, axis=-1, keepdims=True)
        y_ref[:, gs] = (y * lax.rsqrt(ms_ + EPS) * nw_ref[:, gs]).astype(y_ref.dtype)


def _ssd_prompt(xbc, z, dt, conv_w, conv_b, dt_bias, a_log, d_skip_e, norm_w, e64, e128, batch, seq):
    m, conv_dim = xbc.shape
    d_inner = z.shape[1]
    n_heads = dt.shape[1]
    nc = seq // SSD_CHUNK
    full = lambda a: pl.BlockSpec(a.shape, lambda b, c: (0,) * a.ndim)
    row = lambda n: pl.BlockSpec((SSD_CHUNK, n), lambda b, c: (b * nc + c, 0))
    return pl.pallas_call(
        functools.partial(_ssd_prompt_kernel, d_inner=d_inner),
        grid=(batch, nc),
        in_specs=[row(conv_dim), row(d_inner), row(n_heads), full(conv_w), full(conv_b), full(dt_bias),
                  full(a_log), full(d_skip_e), full(norm_w), full(e64), full(e128)],
        out_specs=[row(d_inner),
                   pl.BlockSpec((1, n_heads * SSM_HEAD_DIM, D_STATE), lambda b, c: (b, 0, 0))],
        out_shape=[jax.ShapeDtypeStruct((m, d_inner), BF16),
                   jax.ShapeDtypeStruct((batch, n_heads * SSM_HEAD_DIM, D_STATE), F32)],
        scratch_shapes=[pltpu.VMEM((conv_dim // LANES, SSD_CHUNK + SUBLANES, LANES), F32),
                        pltpu.VMEM((SSD_CHUNK, conv_dim), F32)],
        compiler_params=_cparams(("arbitrary", "arbitrary")),
        name="ssd_prompt",
    )(xbc, z, dt, conv_w, conv_b, dt_bias, a_log, d_skip_e, norm_w, e64, e128)


def _ssd_step_kernel(xbc_ref, z_ref, dt_ref, cst_ref, st_ref, cw_ref, cb_ref, dtb_ref, alog_ref, dsk_ref,
                     nw_ref, e64_ref, e128_ref, y_ref, cst_out_ref, st_out_ref, *, d_inner):
    n_heads = dt_ref.shape[2]
    hpg = n_heads // SSM_GROUPS
    gw = hpg * SSM_HEAD_DIM
    a = -jnp.exp(alog_ref[...])
    row0 = lax.broadcasted_iota(jnp.int32, (SUBLANES, D_STATE), 0) == 0
    b_off = d_inner
    c_off = d_inner + SSM_GROUPS * D_STATE
    for r in range(xbc_ref.shape[0]):
        x_new = xbc_ref[r]
        prev = cst_ref[r]
        acc = cb_ref[...] + cw_ref[CONV_W - 1:CONV_W, :] * x_new
        for k in range(CONV_W - 1):
            acc = acc + cw_ref[k:k + 1, :] * prev[k:k + 1, :]
        xc = _silu(acc)
        cst_out_ref[r, 0:CONV_W - 2, :] = prev[1:CONV_W - 1, :]
        cst_out_ref[r, CONV_W - 2:CONV_W - 1, :] = x_new

        dt = _softplus(dt_ref[r] + dtb_ref[...])
        dec = jnp.exp(dt * a)
        dt_e = _split_dot(jnp.broadcast_to(dt, (SUBLANES, n_heads)), e64_ref[...])[0:1, :]
        dec_rows = _split_dot(jnp.broadcast_to(dec, (SUBLANES, n_heads)), e128_ref[...])
        for g in range(SSM_GROUPS):
            gs = slice(g * gw, (g + 1) * gw)
            xs = xc[:, gs]
            xdt8 = jnp.broadcast_to(xs * dt_e[:, gs], (SUBLANES, gw)).astype(BF16)
            bg = xc[:, b_off + g * D_STATE:b_off + (g + 1) * D_STATE]
            cg = xc[:, c_off + g * D_STATE:c_off + (g + 1) * D_STATE]
            b8 = jnp.where(row0, jnp.broadcast_to(bg, (SUBLANES, D_STATE)), 0.0).astype(BF16)
            c8 = jnp.broadcast_to(cg, (SUBLANES, D_STATE)).astype(BF16)
            upd = lax.dot_general(xdt8, b8, (((0,), (0,)), ((), ())), preferred_element_type=F32)
            news = []
            for hh in range(hpg):
                h = g * hpg + hh
                rs = slice(g * gw + hh * SSM_HEAD_DIM, g * gw + (hh + 1) * SSM_HEAD_DIM)
                new = (st_ref[r, rs, :] * dec_rows[0:1, h * D_STATE:(h + 1) * D_STATE]
                       + upd[hh * SSM_HEAD_DIM:(hh + 1) * SSM_HEAD_DIM, :])
                st_out_ref[r, rs, :] = new
                news.append(new)
            st_new = jnp.concatenate(news, axis=0).astype(BF16)
            y = lax.dot_general(c8, st_new, (((1,), (1,)), ((), ())), preferred_element_type=F32)[0:1, :]
            y = y + xs * dsk_ref[:, gs]
            y = y * _silu(z_ref[r, :, gs])
            ms_ = jnp.mean(y * y, axis=-1, keepdims=True)
            y_ref[r, :, gs] = (y * lax.rsqrt(ms_ + EPS) * nw_ref[:, gs]).astype(y_ref.dtype)


def _ssd_step(xbc, z, dt, conv_state, ssm_state, conv_w, conv_b, dt_bias, a_log, d_skip_e, norm_w, e64, e128):
    bsz, conv_dim = xbc.shape
    d_inner = z.shape[1]
    n_heads = dt.shape[1]
    rows = n_heads * SSM_HEAD_DIM
    full = lambda a: pl.BlockSpec(a.shape, lambda b: (0,) * a.ndim)
    per_b = lambda *s: pl.BlockSpec((STEP_SEQS,) + s, lambda b: (b,) + (0,) * len(s))
    return pl.pallas_call(
        functools.partial(_ssd_step_kernel, d_inner=d_inner),
        grid=(bsz // STEP_SEQS,),
        in_specs=[per_b(1, conv_dim), per_b(1, d_inner), per_b(1, n_heads), per_b(CONV_W - 1, conv_dim),
                  per_b(rows, D_STATE), full(conv_w), full(conv_b), full(dt_bias), full(a_log),
                  full(d_skip_e), full(norm_w), full(e64), full(e128)],
        out_specs=[per_b(1, d_inner), per_b(CONV_W - 1, conv_dim), per_b(rows, D_STATE)],
        out_shape=[jax.ShapeDtypeStruct((bsz, 1, d_inner), BF16),
                   jax.ShapeDtypeStruct((bsz, CONV_W - 1, conv_dim), F32),
                   jax.ShapeDtypeStruct((bsz, rows, D_STATE), F32)],
        compiler_params=_cparams(("arbitrary",)),
        name="ssd_step",
    )(xbc.reshape(bsz, 1, conv_dim), z.reshape(bsz, 1, d_inner), dt.reshape(bsz, 1, n_heads), conv_state,
      ssm_state.reshape(bsz, rows, D_STATE), conv_w, conv_b, dt_bias, a_log, d_skip_e, norm_w, e64, e128)


def _attn_kernel(pt_ref, slopes_ref, lam_ref,
                 q_ref, k_ref, v_ref, g_ref, sw_ref,
                 kc_ref, vc_ref, qd_ref, knd_ref, vnd_ref, gd_ref, slope_ref, e_ref,
                 o_ref, od_ref,
                 kb_ref, vb_ref, qs_ref, m_ref, a_ref,
                 kbuf_ref, vbuf_ref, ksem_ref, vsem_ref, qb_ref, mk_ref, md_ref, ld_ref, accd_ref,
                 *, out_scale, n_pages, page_size):
    t = ATT_BLOCK
    npp = PAGES_PER_STEP
    nslot = DECODE_SLOTS
    seq = q_ref.shape[0]
    nblk = seq // t
    n_hc, width = qb_ref.shape
    n_heads = n_hc // 2
    steps_per_seq = n_pages // npp
    g_idx = pl.program_id(0) * pl.num_programs(1) + pl.program_id(1)
    n_dsteps = pl.num_programs(0) * pl.num_programs(1) * nblk
    slope = slopes_ref[pl.program_id(1)] * LOG2E
    lam = lam_ref[0]

    def page_copies(dstep):
        slot = dstep % nslot
        bs = dstep // steps_per_seq
        s = dstep % steps_per_seq
        cps = []
        for i in range(npp):
            page = pt_ref[bs, s * npp + i]
            cps.append(pltpu.make_async_copy(kc_ref.at[page], kbuf_ref.at[slot, i], ksem_ref.at[slot]))
            cps.append(pltpu.make_async_copy(vc_ref.at[page], vbuf_ref.at[slot, i], vsem_ref.at[slot]))
        return cps

    def start_pages(dstep):
        for cp in page_copies(dstep):
            cp.start()

    def wait_pages(dstep):
        for cp in page_copies(dstep):
            cp.wait()

    @pl.when(g_idx == 0)
    def _():
        for d0 in range(nslot - 1):
            start_pages(d0)
        r = lax.broadcasted_iota(jnp.int32, (n_hc, page_size * n_heads), 0)
        ln = lax.broadcasted_iota(jnp.int32, (n_hc, page_size * n_heads), 1)
        mk_ref[...] = jnp.where(ln % n_heads == r % n_heads, 1.0, 0.0).astype(BF16)
        vb_ref[:, ATT_V_DIM:2 * ATT_V_DIM] = jnp.ones((seq, ATT_V_DIM), BF16)

    def decode_init(dstep):
        @pl.when(dstep % steps_per_seq == 0)
        def _():
            r = lax.broadcasted_iota(jnp.int32, (n_hc, width), 0)
            ln = lax.broadcasted_iota(jnp.int32, (n_hc, width), 1)
            hc = 2 * (r % n_heads) + r // n_heads
            sel = ln // ATT_HEAD_DIM == hc
            qrow = jnp.broadcast_to(qd_ref[0].astype(F32), (n_hc, width))
            qblk = jnp.where(sel, qrow, 0.0)
            qb_ref[...] = qblk.astype(BF16)
            s_new = jnp.sum(qblk * knd_ref[0], axis=1, keepdims=True)
            md_ref[...] = jnp.broadcast_to(s_new, (n_hc, LANES))
            ld_ref[...] = jnp.ones((n_hc, LANES), F32)
            vn = vnd_ref[0]
            accd_ref[...] = jnp.concatenate([vn, vn], axis=0)

    def decode_main(dstep):
        s_idx = dstep % steps_per_seq
        slot = dstep % nslot
        qblk = qb_ref[...]
        srow = slope_ref[...]
        kofs = lax.broadcasted_iota(jnp.int32, (1, page_size), 1)
        kt = jnp.concatenate([kbuf_ref[slot, i].reshape(width, page_size).astype(BF16) for i in range(npp)], axis=1)
        s_all = jnp.dot(qblk, kt, preferred_element_type=F32)
        s_parts = []
        for i in range(npp):
            kpos = (s_idx * npp + i) * page_size + kofs
            dist = (n_pages * page_size - kpos).astype(F32)
            s_parts.append(s_all[:, i * page_size:(i + 1) * page_size] - srow * dist)
        m_prev = md_ref[...]
        m_cur = s_parts[0].max(axis=1, keepdims=True)
        for i in range(1, npp):
            m_cur = jnp.maximum(m_cur, s_parts[i].max(axis=1, keepdims=True))
        m_new = jnp.maximum(m_prev, m_cur)
        alpha = jnp.exp2(m_prev - m_new)
        l_new = alpha * ld_ref[...]
        ps = []
        for i in range(npp):
            p = jnp.exp2(s_parts[i] - m_new)
            l_new = l_new + jnp.sum(p, axis=1, keepdims=True)
            ps.append(p.astype(BF16))
        pe_all = jnp.dot(jnp.concatenate(ps, axis=0), e_ref[...], preferred_element_type=F32).astype(BF16)
        pe_all = pe_all * jnp.concatenate([mk_ref[...]] * npp, axis=0)
        v2 = jnp.concatenate([vbuf_ref[slot, i].reshape(page_size * n_heads, ATT_V_DIM).astype(BF16)
                              for i in range(npp)], axis=1)
        d = jnp.dot(pe_all, v2, preferred_element_type=F32)
        pv = d[0:n_hc, 0:ATT_V_DIM]
        for i in range(1, npp):
            pv = pv + d[i * n_hc:(i + 1) * n_hc, i * ATT_V_DIM:(i + 1) * ATT_V_DIM]
        accd_ref[...] = alpha * accd_ref[...] + pv
        ld_ref[...] = l_new
        md_ref[...] = m_new

    def decode_final(dstep):
        @pl.when(dstep % steps_per_seq == steps_per_seq - 1)
        def _():
            o = (accd_ref[0:n_heads, :] / ld_ref[0:n_heads, :]
                 - lam * (accd_ref[n_heads:n_hc, :] / ld_ref[n_heads:n_hc, :]))
            ms = jnp.mean(o * o, axis=-1, keepdims=True)
            o = o * lax.rsqrt(ms + EPS) * sw_ref[...] * out_scale
            od_ref[0] = (o * _silu(gd_ref[0])).astype(od_ref.dtype)

    for j in range(nblk):
        kb_ref[j] = k_ref[0, 0, :, :, j * t:(j + 1) * t].reshape(2 * ATT_HEAD_DIM, t).astype(BF16)
    vb_ref[:, 0:ATT_V_DIM] = v_ref[...].astype(BF16)
    first = lax.broadcasted_iota(jnp.int32, (t, LANES), 1) < ATT_HEAD_DIM
    kidx = lax.broadcasted_iota(jnp.int32, (1, t), 1).astype(F32)

    def q_block(qi, carry):
        dstep = g_idx * nblk + qi
        nxt = dstep + (nslot - 1)

        @pl.when(nxt < n_dsteps)
        def _():
            start_pages(nxt)

        wait_pages(dstep)
        decode_init(dstep)

        q0 = pl.multiple_of(qi * t, t)
        qp = q_ref[pl.ds(q0, t), :].astype(F32)
        qs_ref[0:t, :] = jnp.where(first, qp, 0.0).astype(BF16)
        qs_ref[t:2 * t, :] = jnp.where(first, 0.0, qp).astype(BF16)

        ri = lax.broadcasted_iota(jnp.int32, (2 * t, t), 0)
        ci = lax.broadcasted_iota(jnp.int32, (2 * t, t), 1)
        causal = ci <= jnp.where(ri >= t, ri - t, ri)
        s = jnp.dot(qs_ref[...], kb_ref[qi], preferred_element_type=F32) + slope * kidx
        s = jnp.where(causal, s, NEG)
        m = jnp.max(s, axis=1, keepdims=True)
        p = jnp.exp2(s - m)
        m_ref[...] = jnp.broadcast_to(m, (2 * t, LANES))
        a_ref[...] = jnp.dot(p.astype(BF16), vb_ref[pl.ds(q0, t), :], preferred_element_type=F32)

        def kv_block(j, carry2):
            k0 = pl.multiple_of(j * t, t)
            bias = slope * (kidx + (k0 - q0).astype(F32))
            s = jnp.dot(qs_ref[...], kb_ref[j], preferred_element_type=F32) + bias
            m_prev = m_ref[...]
            m_new = jnp.maximum(m_prev, jnp.max(s, axis=1, keepdims=True))
            alpha = jnp.exp2(m_prev - m_new)
            p = jnp.exp2(s - jnp.concatenate([m_new] * (t // LANES), axis=1))
            a_ref[...] = (jnp.concatenate([alpha, alpha], axis=1) * a_ref[...]
                          + jnp.dot(p.astype(BF16), vb_ref[pl.ds(k0, t), :], preferred_element_type=F32))
            m_ref[...] = m_new
            return carry2

        lax.fori_loop(0, qi, kv_block, 0)

        decode_main(dstep)
        o = (a_ref[0:t, 0:ATT_V_DIM] / a_ref[0:t, ATT_V_DIM:]
             - lam * (a_ref[t:2 * t, 0:ATT_V_DIM] / a_ref[t:2 * t, ATT_V_DIM:]))
        ms = jnp.mean(o * o, axis=-1, keepdims=True)
        o = o * lax.rsqrt(ms + EPS) * sw_ref[...] * out_scale
        o_ref[pl.ds(q0, t), :] = (o * _silu(g_ref[pl.ds(q0, t), :])).astype(o_ref.dtype)
        decode_final(dstep)
        return carry

    lax.fori_loop(0, nblk, q_block, 0)


def _attn_fused(page_table, slopes, lam, q, kt, v, g, subln_w, kc, vc, qd, knd, vnd, gd, slope_rows, e_tok,
                batch, seq, out_scale):
    m, width = q.shape
    n_heads = width // ATT_V_DIM
    n_hc = 2 * n_heads
    t = ATT_BLOCK
    npp = PAGES_PER_STEP
    bsz, n_pages = page_table.shape
    page_size = kc.shape[-1]
    nblk = seq // t
    assert bsz * (n_pages // npp) == batch * n_heads * nblk, "one decode page group per prompt q tile"
    gps = (n_pages // npp) // nblk
    assert gps * nblk * npp == n_pages
    blk = pl.BlockSpec((seq, ATT_V_DIM), lambda b, h, *_: (b, h))
    dseq = lambda b, h: (b * n_heads + h) // gps
    rowd = pl.BlockSpec((1, 1, width), lambda b, h, *_: (dseq(b, h), 0, 0))
    headd = pl.BlockSpec((1, n_heads, ATT_V_DIM), lambda b, h, *_: (dseq(b, h), 0, 0))
    const = lambda shape: pl.BlockSpec(shape, lambda b, h, *_: (0,) * len(shape))
    grid_spec = pltpu.PrefetchScalarGridSpec(
        num_scalar_prefetch=3,
        grid=(batch, n_heads),
        in_specs=[blk,
                  pl.BlockSpec((1, 1, 2, ATT_HEAD_DIM, seq), lambda b, h, *_: (b, h, 0, 0, 0)),
                  blk, blk, const((1, ATT_V_DIM)),
                  pl.BlockSpec(memory_space=pl.ANY), pl.BlockSpec(memory_space=pl.ANY),
                  rowd, rowd, headd, headd, const((n_hc, LANES)), const((page_size, page_size * n_heads))],
        out_specs=[blk, headd],
        scratch_shapes=[pltpu.VMEM((nblk, 2 * ATT_HEAD_DIM, t), BF16), pltpu.VMEM((seq, 2 * ATT_V_DIM), BF16),
                        pltpu.VMEM((2 * t, LANES), BF16),
                        pltpu.VMEM((2 * t, LANES), F32), pltpu.VMEM((2 * t, 2 * ATT_V_DIM), F32),
                        pltpu.VMEM((DECODE_SLOTS, npp, n_heads, 2, ATT_HEAD_DIM, page_size), F32),
                        pltpu.VMEM((DECODE_SLOTS, npp, page_size, n_heads, ATT_V_DIM), F32),
                        pltpu.SemaphoreType.DMA((DECODE_SLOTS,)), pltpu.SemaphoreType.DMA((DECODE_SLOTS,)),
                        pltpu.VMEM((n_hc, width), BF16), pltpu.VMEM((n_hc, page_size * n_heads), BF16),
                        pltpu.VMEM((n_hc, LANES), F32), pltpu.VMEM((n_hc, LANES), F32),
                        pltpu.VMEM((n_hc, ATT_V_DIM), F32)],
    )
    o, od = pl.pallas_call(
        functools.partial(_attn_kernel, out_scale=out_scale, n_pages=n_pages, page_size=page_size),
        grid_spec=grid_spec,
        out_shape=[jax.ShapeDtypeStruct((m, width), BF16), jax.ShapeDtypeStruct((bsz, n_heads, ATT_V_DIM), BF16)],
        compiler_params=_cparams(("arbitrary", "arbitrary")),
        name="attn_fused",
    )(page_table, slopes, lam, q, kt, v, g, subln_w, kc, vc,
      qd.reshape(bsz, 1, width), knd.reshape(bsz, 1, width), vnd.reshape(bsz, n_heads, ATT_V_DIM),
      gd.reshape(bsz, n_heads, ATT_V_DIM), slope_rows, e_tok)
    return o, od.reshape(bsz, width)


def _expand_matrix(n_rows, rep):
    r = lax.broadcasted_iota(jnp.int32, (n_rows, n_rows * rep), 0)
    c = lax.broadcasted_iota(jnp.int32, (n_rows, n_rows * rep), 1)
    return (c // rep == r).astype(BF16)


def kernel(x_prompt, x_sample, state_conv, state_ssm, cache_k, cache_v, page_table, c_prompt, c_sample,
           ada_w, ada_b, ln_w, ssm_w_in, ssm_conv_w, ssm_conv_b, ssm_dt_bias, ssm_a_log, ssm_d,
           ssm_norm_w, ssm_w_out, att_w_in, att_q_norm, att_k_norm, att_lambda, att_subln_w, att_w_out):
    batch, seq, d = x_prompt.shape
    dec_batch = x_sample.shape[0]
    n_ssm_heads = ssm_a_log.shape[1]
    d_inner = n_ssm_heads * SSM_HEAD_DIM
    conv_dim = ssm_conv_w.shape[2]
    att_width = att_w_out.shape[1]
    n_att_heads = att_width // ATT_V_DIM
    m = batch * seq
    tm = 512
    tm_out = 1024
    tiles_per_seq = seq // tm

    mod = _adaln(jnp.concatenate([c_prompt, c_sample], axis=0), ada_w, ada_b)

    def mods(i):
        parts = [mod[i, :, k * d:(k + 1) * d] for k in range(3)]
        return ([p[:batch].reshape(batch, 1, d) for p in parts],
                [p[batch:].reshape(1, dec_batch, d) for p in parts])

    xp = x_prompt.reshape(m, d)
    xs = x_sample.reshape(dec_batch, d)

    (sh_p, sc_p, gt_p), (sh_s, sc_s, gt_s) = mods(0)
    w_in = ssm_w_in[0].astype(BF16)
    wz, wx, wd = w_in[:, :d_inner], w_in[:, d_inner:d_inner + conv_dim], w_in[:, d_inner + conv_dim:]
    lnw0 = ln_w[0].reshape(1, d)
    conv_w, conv_b = ssm_conv_w[0], ssm_conv_b[0].reshape(1, conv_dim)
    dt_bias, a_log = ssm_dt_bias[0].reshape(1, -1), ssm_a_log[0].reshape(1, -1)
    d_skip_e = jnp.repeat(ssm_d[0], SSM_HEAD_DIM).reshape(1, d_inner)
    norm_w = ssm_norm_w[0].reshape(1, d_inner)
    w_out0 = ssm_w_out[0].astype(BF16)
    e64 = _expand_matrix(n_ssm_heads, SSM_HEAD_DIM)
    e128 = _expand_matrix(n_ssm_heads, D_STATE)

    z_p, xbc_p, dt_p = _inproj_ssm(xp, lnw0, sc_p, sh_p, wz, wx, wd, tm, tiles_per_seq)
    yn_p, ssm_p = _ssd_prompt(xbc_p, z_p, dt_p, conv_w, conv_b, dt_bias, a_log, d_skip_e, norm_w,
                              jnp.concatenate([e64, e64], axis=0), e128, batch, seq)
    y1_p = _outproj(yn_p, w_out0, xp, gt_p, tm_out, seq // tm_out)
    conv_p = xbc_p.reshape(batch, seq, conv_dim)[:, seq - (CONV_W - 1):, :]

    z_s, xbc_s, dt_s = _inproj_ssm(xs, lnw0, sc_s, sh_s, wz, wx, wd, dec_batch, 1)
    yn_s, conv_s, ssm_s = _ssd_step(xbc_s, z_s, dt_s, state_conv[0], state_ssm[0], conv_w, conv_b, dt_bias,
                                    a_log, d_skip_e, norm_w, e64, e128)
    y1_s = _outproj(yn_s.reshape(dec_batch, d_inner), w_out0, xs, gt_s, dec_batch, 1)

    (sh_p, sc_p, gt_p), (sh_s, sc_s, gt_s) = mods(1)
    lam_init = 0.8 - 0.6 * math.exp(-0.3 * 1)
    lmb = att_lambda[0].astype(F32)
    lam = (jnp.exp(jnp.sum(lmb[0] * lmb[1])) - jnp.exp(jnp.sum(lmb[2] * lmb[3])) + lam_init).reshape(1)
    slopes = jnp.exp2(-8.0 * jnp.arange(1, n_att_heads + 1, dtype=F32) / n_att_heads)
    wa = att_w_in[0].astype(BF16)
    wq, wk, wv, wg = (wa[:, i * att_width:(i + 1) * att_width] for i in range(4))
    wkt = wk.T
    lnw1 = ln_w[1].reshape(1, d)
    n_rep = att_width // ATT_HEAD_DIM
    qn = jnp.tile(att_q_norm[0], n_rep).reshape(1, att_width)
    gi = lax.broadcasted_iota(jnp.int32, (2 * LANES, 2 * LANES), 0) // ATT_HEAD_DIM
    gj = lax.broadcasted_iota(jnp.int32, (2 * LANES, 2 * LANES), 1) // ATT_HEAD_DIM
    gsum = (gi == gj).astype(BF16)
    subln_w = att_subln_w[0].reshape(1, ATT_V_DIM)
    w_out1 = att_w_out[0].astype(BF16)
    out_scale = 1.0 - lam_init

    def kn_rep(cols):
        return jnp.broadcast_to(att_k_norm[0][:, None], (ATT_HEAD_DIM, cols))

    q_p, kt_p, v_p, g_p = _inproj_att(y1_p, lnw1, sc_p, sh_p, wq, wkt, wv, wg, qn, kn_rep(tm), gsum, tm,
                                      tiles_per_seq)
    kt_p = kt_p.reshape(batch, n_att_heads, 2, ATT_HEAD_DIM, seq)

    q_s, kt_s, v_s, g_s = _inproj_att(y1_s, lnw1, sc_s, sh_s, wq, wkt, wv, wg, qn, kn_rep(dec_batch), gsum,
                                      dec_batch, 1)
    k_s = kt_s[0].T
    page_size = cache_k.shape[2]
    n_hc = 2 * n_att_heads
    slope_rows = jnp.broadcast_to((jnp.tile(slopes, 2) * LOG2E)[:, None], (n_hc, LANES))
    ei = lax.broadcasted_iota(jnp.int32, (page_size, page_size * n_att_heads), 0)
    ej = lax.broadcasted_iota(jnp.int32, (page_size, page_size * n_att_heads), 1) // n_att_heads
    e_tok = (ei == ej).astype(BF16)
    kc = jnp.transpose(cache_k[0], (0, 2, 3, 4, 1))
    o_p, o_s = _attn_fused(page_table, slopes, lam, q_p, kt_p, v_p, g_p, subln_w, kc, cache_v[0],
                           q_s, k_s, v_s, g_s, slope_rows, e_tok, batch, seq, out_scale)
    y2_p = _outproj(o_p, w_out1, y1_p, gt_p, tm_out, seq // tm_out)
    y2_s = _outproj(o_s, w_out1, y1_s, gt_s, dec_batch, 1)
    k_p = jnp.transpose(kt_p, (0, 4, 1, 2, 3))

    return (y2_p.reshape(batch, seq, d),
            y2_s.reshape(dec_batch, 1, d),
            conv_p[None],
            ssm_p.reshape(1, batch, n_ssm_heads, SSM_HEAD_DIM, D_STATE),
            k_p[None],
            v_p.reshape(1, batch, seq, n_att_heads, ATT_V_DIM),
            conv_s[None],
            ssm_s.reshape(1, dec_batch, n_ssm_heads, SSM_HEAD_DIM, D_STATE),
            k_s.reshape(1, dec_batch, 1, n_att_heads, 2, ATT_HEAD_DIM),
            v_s.reshape(1, dec_batch, 1, n_att_heads, ATT_V_DIM))
```
